```python
import math
import jax, jax.numpy as jnp
from jax import lax
import numpy as np

D_MODEL = 2048
BATCH = 2
SEQ = 4096
DEPTH = 4
DEC_BATCH = 8
DEC_SEQ = 2048
PAST_LEN = 128

HEAD_DIM = 64
V_DIM = 2 * HEAD_DIM
ATTN_WIDTH = D_MODEL // 2
N_HEADS = ATTN_WIDTH // V_DIM
FOURIER_WIDTH = D_MODEL - ATTN_WIDTH
N_FGROUPS = 4
FGROUP_DIM = FOURIER_WIDTH // N_FGROUPS
AB_IN_WIDTH = 3 * ATTN_WIDTH + FOURIER_WIDTH
CONV_DIM = D_MODEL
CONV_K = 3
D_FF = (11 * D_MODEL) // 4
NUM_BUCKETS = 32
MAX_DISTANCE = 128
Q_BLOCK = 128
N_AB_LAYERS = (DEPTH + 1) // 2
N_C_LAYERS = DEPTH // 2
EPS = 1e-6

kernel_name = "hybrid_diffattn_fnet_shortconv_encoder"


def rms_norm(x, g):
    xf = x.astype(jnp.float32)
    xf = xf * lax.rsqrt(jnp.mean(xf * xf, axis=-1, keepdims=True) + EPS)
    return (xf * g.astype(jnp.float32)).astype(x.dtype)


def dwconv3(x, w):
    xp = jnp.pad(x, ((0, 0), (1, 1), (0, 0)))
    return xp[:, :-2] * w[0] + xp[:, 1:-1] * w[1] + xp[:, 2:] * w[2]


def rel_bucket(rel):
    nb = NUM_BUCKETS // 2
    ret = jnp.where(rel > 0, nb, 0)
    n = jnp.abs(rel)
    max_exact = nb // 2
    nf = jnp.maximum(n, 1).astype(jnp.float32)
    large = max_exact + (jnp.log(nf / max_exact) / math.log(MAX_DISTANCE / max_exact)
                         * (nb - max_exact)).astype(jnp.int32)
    large = jnp.minimum(large, nb - 1)
    return ret + jnp.where(n < max_exact, n, large)


def lambda_init_fn(layer_idx):
    return 0.8 - 0.6 * math.exp(-0.3 * layer_idx)


def diff_attention(q, k, v, rel_bias, lam):
    B, S = q.shape[0], q.shape[1]
    nblk = S // Q_BLOCK
    qb = q.reshape(B, nblk, Q_BLOCK, N_HEADS, 2, HEAD_DIM).transpose(1, 0, 2, 3, 4, 5)
    starts = jnp.arange(nblk, dtype=jnp.int32) * Q_BLOCK
    kpos = jnp.arange(S, dtype=jnp.int32)
    scale = HEAD_DIM ** -0.5

    def one_block(args):
        qblk, start = args
        qpos = start + jnp.arange(Q_BLOCK, dtype=jnp.int32)
        bucket = rel_bucket(kpos[None, :] - qpos[:, None])
        bias = jnp.transpose(rel_bias[bucket], (2, 0, 1)).astype(jnp.float32)
        s = jnp.einsum('bqhmd,bkhmd->bhmqk', qblk, k).astype(jnp.float32) * scale
        s = s + bias[None, :, None]
        p = jax.nn.softmax(s, axis=-1)
        a = p[:, :, 0] - lam * p[:, :, 1]
        return jnp.einsum('bhqk,bkhe->bqhe', a.astype(v.dtype), v)

    o = lax.map(one_block, (qb, starts))
    return o.transpose(1, 0, 2, 3, 4).reshape(B, S, N_HEADS, V_DIM)


def mixer_ab(h, w_in, w_out, lq1, lk1, lq2, lk2, subln_g, rel_bias, lambda_init):
    B, S, _ = h.shape
    z = h @ w_in
    q = z[..., :ATTN_WIDTH].reshape(B, S, N_HEADS, 2, HEAD_DIM)
    k = z[..., ATTN_WIDTH:2 * ATTN_WIDTH].reshape(B, S, N_HEADS, 2, HEAD_DIM)
    v = z[..., 2 * ATTN_WIDTH:3 * ATTN_WIDTH].reshape(B, S, N_HEADS, V_DIM)
    f = z[..., 3 * ATTN_WIDTH:]
    lam = (jnp.exp(jnp.sum(lq1.astype(jnp.float32) * lk1.astype(jnp.float32)))
           - jnp.exp(jnp.sum(lq2.astype(jnp.float32) * lk2.astype(jnp.float32)))
           + lambda_init)
    o = diff_attention(q, k, v, rel_bias, lam)
    o = (rms_norm(o, subln_g) * (1.0 - lambda_init)).reshape(B, S, ATTN_WIDTH)
    fg = f.reshape(B, S, N_FGROUPS, FGROUP_DIM).astype(jnp.float32)
    fo = jnp.real(jnp.fft.fft2(fg, axes=(1, 3), norm='ortho'))
    fo = fo.astype(h.dtype).reshape(B, S, FOURIER_WIDTH)
    return jnp.concatenate([o, fo], axis=-1) @ w_out


def mixer_c(h, w_in, conv_w, w_out):
    z = h @ w_in
    bg, cg, xv = jnp.split(z, 3, axis=-1)
    return (bg * dwconv3(cg * xv, conv_w)) @ w_out


def conv_ffn(h, w_gate, w_up, conv_w, conv_b, w_down):
    g = dwconv3(h @ w_gate, conv_w) + conv_b
    return (jax.nn.silu(g) * (h @ w_up)) @ w_down


def run_trunk(x, rel_bias, norm_pre_mix, norm_post_mix, norm_pre_ffn, norm_post_ffn,
              ab_w_in, ab_w_out, ab_lambda_q1, ab_lambda_k1, ab_lambda_q2, ab_lambda_k2, ab_subln,
              c_w_in, c_conv, c_w_out, ffn_w_gate, ffn_w_up, ffn_conv, ffn_conv_b, ffn_w_down):
    for i in range(DEPTH):
        j = i // 2
        h = rms_norm(x, norm_pre_mix[i])
        if i % 2 == 0:
            m = mixer_ab(h, ab_w_in[j], ab_w_out[j], ab_lambda_q1[j], ab_lambda_k1[j],
                         ab_lambda_q2[j], ab_lambda_k2[j], ab_subln[j], rel_bias,
                         lambda_init_fn(i))
        else:
            m = mixer_c(h, c_w_in[j], c_conv[j], c_w_out[j])
        x = x + rms_norm(m, norm_post_mix[i])
        h = rms_norm(x, norm_pre_ffn[i])
        f = conv_ffn(h, ffn_w_gate[i], ffn_w_up[i], ffn_conv[i], ffn_conv_b[i], ffn_w_down[i])
        x = x + rms_norm(f, norm_post_ffn[i])
    return x


def setup_inputs(seed: int = 0) -> dict:
    key = jax.random.key(seed)
    ks = jax.random.split(key, 24)
    f32 = jnp.float32

    def nrm(k, shape, scale):
        return jax.random.normal(k, shape, f32) * scale

    def gain(k, shape):
        return 1.0 + 0.05 * jax.random.normal(k, shape, f32)

    return {
        "x_prompt": nrm(ks[0], (BATCH, SEQ, D_MODEL), 1.0),
        "x_sample": nrm(ks[1], (DEC_BATCH, DEC_SEQ, D_MODEL), 1.0),
        "rel_bias": nrm(ks[2], (NUM_BUCKETS, N_HEADS), 0.5),
        "norm_pre_mix": gain(ks[3], (DEPTH, D_MODEL)),
        "norm_post_mix": gain(ks[4], (DEPTH, D_MODEL)),
        "norm_pre_ffn": gain(ks[5], (DEPTH, D_MODEL)),
        "norm_post_ffn": gain(ks[6], (DEPTH, D_MODEL)),
        "ab_w_in": nrm(ks[7], (N_AB_LAYERS, D_MODEL, AB_IN_WIDTH), D_MODEL ** -0.5),
        "ab_w_out": nrm(ks[8], (N_AB_LAYERS, D_MODEL, D_MODEL), D_MODEL ** -0.5),
        "ab_lambda_q1": nrm(ks[9], (N_AB_LAYERS, HEAD_DIM), 0.1),
        "ab_lambda_k1": nrm(ks[10], (N_AB_LAYERS, HEAD_DIM), 0.1),
        "ab_lambda_q2": nrm(ks[11], (N_AB_LAYERS, HEAD_DIM), 0.1),
        "ab_lambda_k2": nrm(ks[12], (N_AB_LAYERS, HEAD_DIM), 0.1),
        "ab_subln": gain(ks[13], (N_AB_LAYERS, V_DIM)),
        "c_w_in": nrm(ks[14], (N_C_LAYERS, D_MODEL, 3 * CONV_DIM), D_MODEL ** -0.5),
        "c_conv": nrm(ks[15], (N_C_LAYERS, CONV_K, CONV_DIM), CONV_K ** -0.5),
        "c_w_out": nrm(ks[16], (N_C_LAYERS, CONV_DIM, D_MODEL), CONV_DIM ** -0.5),
        "ffn_w_gate": nrm(ks[17], (DEPTH, D_MODEL, D_FF), D_MODEL ** -0.5),
        "ffn_w_up": nrm(ks[18], (DEPTH, D_MODEL, D_FF), D_MODEL ** -0.5),
        "ffn_conv": nrm(ks[19], (DEPTH, CONV_K, D_FF), CONV_K ** -0.5),
        "ffn_conv_b": nrm(ks[20], (DEPTH, D_FF), 0.02),
        "ffn_w_down": nrm(ks[21], (DEPTH, D_FF, D_MODEL), D_FF ** -0.5),
    }


def reference(x_prompt, x_sample, rel_bias, norm_pre_mix, norm_post_mix, norm_pre_ffn, norm_post_ffn,
              ab_w_in, ab_w_out, ab_lambda_q1, ab_lambda_k1, ab_lambda_q2, ab_lambda_k2, ab_subln,
              c_w_in, c_conv, c_w_out, ffn_w_gate, ffn_w_up, ffn_conv, ffn_conv_b, ffn_w_down):
    y_prompt = run_trunk(x_prompt, rel_bias, norm_pre_mix, norm_post_mix, norm_pre_ffn, norm_post_ffn,
                         ab_w_in, ab_w_out, ab_lambda_q1, ab_lambda_k1, ab_lambda_q2, ab_lambda_k2,
                         ab_subln, c_w_in, c_conv, c_w_out, ffn_w_gate, ffn_w_up, ffn_conv,
                         ffn_conv_b, ffn_w_down)
    y_sample = run_trunk(x_sample, rel_bias, norm_pre_mix, norm_post_mix, norm_pre_ffn, norm_post_ffn,
                         ab_w_in, ab_w_out, ab_lambda_q1, ab_lambda_k1, ab_lambda_q2, ab_lambda_k2,
                         ab_subln, c_w_in, c_conv, c_w_out, ffn_w_gate, ffn_w_up, ffn_conv,
                         ffn_conv_b, ffn_w_down)
    return (y_prompt, y_sample)
```

```python
import functools
import math

import jax
import jax.numpy as jnp
from jax import lax
from jax.experimental import pallas as pl
from jax.experimental.pallas import tpu as pltpu

F32 = jnp.float32
BF16 = jnp.bfloat16

EPS = 1e-6
MAX_DISTANCE = 128
N_FGROUPS = 4
V7X_VMEM_LIMIT_BYTES = 56 * 1024 * 1024
HALO_ROWS = 8
ATTN_TILE = 256
NEG_BIG = -1e30


def _params(*semantics):
    return pltpu.CompilerParams(dimension_semantics=semantics,
                                vmem_limit_bytes=V7X_VMEM_LIMIT_BYTES)


def _rms(x, g):
    ms = jnp.mean(x * x, axis=-1, keepdims=True)
    return (x * lax.rsqrt(ms + EPS)) * g


def _lambda_init(layer_idx):
    return 0.8 - 0.6 * math.exp(-0.3 * layer_idx)


def _conv_rows(t, cw, tm):
    n = t.shape[0]
    prev = pltpu.roll(t, 1, 0)[:tm]
    nxt = pltpu.roll(t, n - 1, 0)[:tm]
    return prev * cw[0:1] + t[:tm] * cw[1:2] + nxt * cw[2:3]


def _gated_block_kernel(xp_ref, x_ref, xn_ref, gpre_ref, *rest, mode, tm, tiles_per_seq):
    if mode == "ffn":
        wg_ref, wu_ref, cw_ref, cb_ref, wd_ref, gpost_ref, o_ref, h_scr, acc_scr = rest
    else:
        wb_ref, wc_ref, wx_ref, cw_ref, wd_ref, gpost_ref, o_ref, h_scr, acc_scr = rest
    i = pl.program_id(0)
    j = pl.program_id(1)
    nj = pl.num_programs(1)

    @pl.when(j == 0)
    def _():
        g = gpre_ref[...]
        h_scr[0:tm, :] = _rms(x_ref[...], g).astype(BF16)
        it = i % tiles_per_seq
        keep_prev = jnp.where(it != 0, 1.0, 0.0)
        keep_next = jnp.where(it != tiles_per_seq - 1, 1.0, 0.0)
        halo = jnp.concatenate([_rms(xn_ref[...], g) * keep_next,
                                _rms(xp_ref[...], g) * keep_prev], axis=0)
        h_scr[tm:tm + 2 * HALO_ROWS, :] = halo.astype(BF16)
        acc_scr[...] = jnp.zeros_like(acc_scr)

    h_all = h_scr[...]
    h_main = h_scr[0:tm, :]
    cw = cw_ref[...]
    if mode == "ffn":
        gate = jnp.dot(h_all, wg_ref[...], preferred_element_type=F32)
        up = jnp.dot(h_main, wu_ref[...], preferred_element_type=F32)
        g = _conv_rows(gate, cw, tm) + cb_ref[...]
        a = (g * jax.nn.sigmoid(g)) * up
    else:
        bg = jnp.dot(h_main, wb_ref[...], preferred_element_type=F32)
        cg = jnp.dot(h_all, wc_ref[...], preferred_element_type=F32)
        xv = jnp.dot(h_all, wx_ref[...], preferred_element_type=F32)
        a = bg * _conv_rows(cg * xv, cw, tm)
    acc_scr[...] += jnp.dot(a.astype(BF16), wd_ref[...], preferred_element_type=F32)

    @pl.when(j == nj - 1)
    def _():
        o_ref[...] = x_ref[...] + _rms(acc_scr[...], gpost_ref[...])


def _gated_block(x, seq_len, gpre, gpost, w_in_list, conv_w, conv_b, w_out, *, mode, tm, tf):
    M, D = x.shape
    C = w_out.shape[0]
    assert M % tm == 0 and seq_len % tm == 0 and C % tf == 0 and tm % (2 * HALO_ROWS) == 0
    ni, nj = M // tm, C // tf
    tiles_per_seq = seq_len // tm
    blocks_per_tile = tm // HALO_ROWS
    last_block = M // HALO_ROWS - 1

    row = lambda i, j: (i, 0)
    in_specs = [
        pl.BlockSpec((HALO_ROWS, D), lambda i, j: (jnp.maximum(i * blocks_per_tile - 1, 0), 0)),
        pl.BlockSpec((tm, D), row),
        pl.BlockSpec((HALO_ROWS, D), lambda i, j: (jnp.minimum((i + 1) * blocks_per_tile, last_block), 0)),
        pl.BlockSpec((1, D), lambda i, j: (0, 0)),
    ]
    args = [x, x, x, gpre.reshape(1, D)]
    if mode == "ffn":
        wg, wu = w_in_list
        in_specs += [pl.BlockSpec((D, tf), lambda i, j: (0, j)),
                     pl.BlockSpec((D, tf), lambda i, j: (0, j)),
                     pl.BlockSpec((3, tf), lambda i, j: (0, j)),
                     pl.BlockSpec((1, tf), lambda i, j: (0, j))]
        args += [wg, wu, conv_w, conv_b.reshape(1, C)]
    else:
        (w_in,) = w_in_list
        in_specs += [pl.BlockSpec((D, tf), lambda i, j: (0, j)),
                     pl.BlockSpec((D, tf), lambda i, j: (0, nj + j)),
                     pl.BlockSpec((D, tf), lambda i, j: (0, 2 * nj + j)),
                     pl.BlockSpec((3, tf), lambda i, j: (0, j))]
        args += [w_in, w_in, w_in, conv_w]
    in_specs += [pl.BlockSpec((tf, D), lambda i, j: (j, 0)),
                 pl.BlockSpec((1, D), lambda i, j: (0, 0))]
    args += [w_out, gpost.reshape(1, D)]

    return pl.pallas_call(
        functools.partial(_gated_block_kernel, mode=mode, tm=tm, tiles_per_seq=tiles_per_seq),
        grid=(ni, nj),
        in_specs=in_specs,
        out_specs=pl.BlockSpec((tm, D), row),
        out_shape=jax.ShapeDtypeStruct((M, D), F32),
        scratch_shapes=[pltpu.VMEM((tm + 2 * HALO_ROWS, D), BF16),
                        pltpu.VMEM((tm, D), F32)],
        compiler_params=_params("parallel", "arbitrary"),
        name="gated_block_" + mode,
    )(*args)


def _norm_matmul_kernel(x_ref, g_ref, w_ref, o_ref, h_scr):
    @pl.when(pl.program_id(1) == 0)
    def _():
        h_scr[...] = _rms(x_ref[...], g_ref[...]).astype(BF16)

    o_ref[...] = jnp.dot(h_scr[...], w_ref[...], preferred_element_type=F32).astype(o_ref.dtype)


def _norm_matmul(x, g, w, *, tm, tn):
    M, D = x.shape
    N = w.shape[1]
    assert M % tm == 0 and N % tn == 0
    return pl.pallas_call(
        _norm_matmul_kernel,
        grid=(M // tm, N // tn),
        in_specs=[pl.BlockSpec((tm, D), lambda i, j: (i, 0)),
                  pl.BlockSpec((1, D), lambda i, j: (0, 0)),
                  pl.BlockSpec((D, tn), lambda i, j: (0, j))],
        out_specs=pl.BlockSpec((tm, tn), lambda i, j: (i, j)),
        out_shape=jax.ShapeDtypeStruct((M, N), BF16),
        scratch_shapes=[pltpu.VMEM((tm, D), BF16)],
        compiler_params=_params("parallel", "arbitrary"),
        name="norm_in_proj",
    )(x, g.reshape(1, D), w)


def _bucket_saturation_distance(num_buckets):
    nb = num_buckets // 2
    max_exact = nb // 2
    n = max_exact
    while max_exact + int(math.log(n / max_exact) / math.log(MAX_DISTANCE / max_exact)
                          * (nb - max_exact)) < nb - 1:
        n += 1
    return n


def _attn_consts_kernel(rb_ref, lq1_ref, lk1_ref, lq2_ref, lk2_ref, bias_ref, lam_ref, *, T, num_buckets):
    h = pl.program_id(0)
    nb = num_buckets // 2
    max_exact = nb // 2
    row = lax.broadcasted_iota(jnp.int32, (T, T), 0)
    col = lax.broadcasted_iota(jnp.int32, (T, T), 1)
    for t, off in enumerate((-2, -1, 0, 1, 2)):
        rel = col - row + off * T
        ret = jnp.where(rel > 0, nb, 0)
        n = jnp.abs(rel)
        nf = jnp.maximum(n, 1).astype(F32)
        large = max_exact + (jnp.log(nf / max_exact) / math.log(MAX_DISTANCE / max_exact)
                             * (nb - max_exact)).astype(jnp.int32)
        large = jnp.minimum(large, nb - 1)
        bucket = ret + jnp.where(n < max_exact, n, large)
        bias = jnp.zeros((T, T), F32)
        for b in range(num_buckets):
            bias = jnp.where(bucket == b, rb_ref[b, h], bias)
        bias_ref[0, t] = bias
    s1 = jnp.sum(lq1_ref[...] * lk1_ref[...], axis=-1, keepdims=True)
    s2 = jnp.sum(lq2_ref[...] * lk2_ref[...], axis=-1, keepdims=True)
    lam_ref[...] = jnp.broadcast_to(jnp.exp(s1) - jnp.exp(s2), lam_ref.shape)


def _attn_consts(rel_bias, lq1, lk1, lq2, lk2, *, T):
    num_buckets, H = rel_bias.shape
    n_ab = lq1.shape[0]
    assert T + 1 >= _bucket_saturation_distance(num_buckets)
    vec = pl.BlockSpec(lq1.shape, lambda h: (0, 0))
    return pl.pallas_call(
        functools.partial(_attn_consts_kernel, T=T, num_buckets=num_buckets),
        grid=(H,),
        in_specs=[pl.BlockSpec(memory_space=pltpu.SMEM), vec, vec, vec, vec],
        out_specs=[pl.BlockSpec((1, 5, T, T), lambda h: (h, 0, 0, 0)),
                   pl.BlockSpec((n_ab, 128), lambda h: (0, 0))],
        out_shape=[jax.ShapeDtypeStruct((H, 5, T, T), F32),
                   jax.ShapeDtypeStruct((n_ab, 128), F32)],
        compiler_params=_params("arbitrary"),
        name="attn_consts",
    )(rel_bias, lq1, lk1, lq2, lk2)


def _attn_kernel(lam_ref, q_ref, k_ref, v_ref, bias_ref, sub_ref, o_ref,
                 m1, l1, a1, m2, l2, a2, *, T, nk, head_dim, out_scale):
    qi = pl.program_id(2)
    q = q_ref[...].astype(F32) * (head_dim ** -0.5)
    lane = lax.broadcasted_iota(jnp.int32, q.shape, 1)
    q1 = jnp.where(lane < head_dim, q, 0.0).astype(BF16)
    q2 = jnp.where(lane >= head_dim, q, 0.0).astype(BF16)
    maps = ((q1, m1, l1, a1), (q2, m2, l2, a2))
    for _, m, l, a in maps:
        m[...] = jnp.full_like(m, NEG_BIG)
        l[...] = jnp.zeros_like(l)
        a[...] = jnp.zeros_like(a)

    def body(kj, carry):
        start = pl.multiple_of(kj * T, T)
        kt = k_ref[pl.ds(start, T), :]
        vt = v_ref[pl.ds(start, T), :]
        b = bias_ref[0, jnp.clip(kj - qi, -2, 2) + 2]
        for qm, m, l, a in maps:
            s = lax.dot_general(qm, kt, (((1,), (1,)), ((), ())),
                                preferred_element_type=F32) + b
            m_old = m[...]
            m_new = jnp.maximum(m_old, jnp.max(s, axis=-1, keepdims=True))
            alpha = jnp.exp(m_old - m_new)
            e = jnp.exp(s - m_new)
            l[...] = alpha * l[...] + jnp.sum(e, axis=-1, keepdims=True)
            a[...] = alpha * a[...] + jnp.dot(e.astype(BF16), vt, preferred_element_type=F32)
            m[...] = m_new
        return carry

    lax.fori_loop(0, nk, body, 0)
    lam = lam_ref[0, 0]
    o = a1[...] / l1[...] - lam * (a2[...] / l2[...])
    o_ref[...] = (_rms(o, sub_ref[...]) * out_scale).astype(o_ref.dtype)


def _diff_attention(z, lam, bias_tiles, subln, *, B, S, H, head_dim, out_scale, T):
    M = z.shape[0]
    vd = 2 * head_dim
    assert vd == 128 and S % T == 0
    nq = S // T
    return pl.pallas_call(
        functools.partial(_attn_kernel, T=T, nk=nq, head_dim=head_dim, out_scale=out_scale),
        grid=(B, H, nq),
        in_specs=[pl.BlockSpec(memory_space=pltpu.SMEM),
                  pl.BlockSpec((T, vd), lambda b, h, i: (b * nq + i, h)),
                  pl.BlockSpec((S, vd), lambda b, h, i: (b, H + h)),
                  pl.BlockSpec((S, vd), lambda b, h, i: (b, 2 * H + h)),
                  pl.BlockSpec((1, 5, T, T), lambda b, h, i: (h, 0, 0, 0)),
                  pl.BlockSpec((1, vd), lambda b, h, i: (0, 0))],
        out_specs=pl.BlockSpec((T, vd), lambda b, h, i: (b * nq + i, h)),
        out_shape=jax.ShapeDtypeStruct((M, H * vd), BF16),
        scratch_shapes=[pltpu.VMEM((T, 1), F32), pltpu.VMEM((T, 1), F32), pltpu.VMEM((T, vd), F32),
                        pltpu.VMEM((T, 1), F32), pltpu.VMEM((T, 1), F32), pltpu.VMEM((T, vd), F32)],
        compiler_params=_params("parallel", "parallel", "arbitrary"),
        name="diff_attention",
    )(lam, z, z, z, bias_tiles, subln.reshape(1, vd))


def _dft_tables(n, scale):
    idx = jnp.arange(n, dtype=jnp.int32)
    ang = ((idx[:, None] * idx[None, :]) % n).astype(F32) * (2.0 * math.pi / n)
    return (jnp.cos(ang) * scale).astype(BF16), (jnp.sin(ang) * scale).astype(BF16)


def _dft_channel_kernel(f_ref, cs_ref, a_ref, b_ref):
    c = f_ref.shape[1]
    y = jnp.dot(f_ref[...], cs_ref[...], preferred_element_type=F32)
    a_ref[...] = y[:, :c].astype(a_ref.dtype)
    b_ref[...] = y[:, c:].astype(b_ref.dtype)


def _dft_channel(z, col0, cs, *, tm):
    M = z.shape[0]
    c = cs.shape[0]
    blk0 = col0 // c
    out = jax.ShapeDtypeStruct((M, N_FGROUPS * c), BF16)
    return pl.pallas_call(
        _dft_channel_kernel,
        grid=(M // tm, N_FGROUPS),
        in_specs=[pl.BlockSpec((tm, c), lambda i, g: (i, blk0 + g)),
                  pl.BlockSpec((c, 2 * c), lambda i, g: (0, 0))],
        out_specs=[pl.BlockSpec((tm, c), lambda i, g: (i, g)),
                   pl.BlockSpec((tm, c), lambda i, g: (i, g))],
        out_shape=[out, out],
        compiler_params=_params("parallel", "parallel"),
        name="dft_channel",
    )(z, cs)


def _dft_seq_kernel(cs_ref, ss_ref, a_ref, b_ref, o_ref):
    y = (jnp.dot(cs_ref[...], a_ref[...], preferred_element_type=F32)
         - jnp.dot(ss_ref[...], b_ref[...], preferred_element_type=F32))
    o_ref[...] = y.astype(o_ref.dtype)


def _dft_seq(a, b, cos_s, sin_s, *, B, S, tm, tn):
    M, N = a.shape
    ni = S // tm
    return pl.pallas_call(
        _dft_seq_kernel,
        grid=(B, N // tn, ni),
        in_specs=[pl.BlockSpec((tm, S), lambda bb, n, i: (i, 0)),
                  pl.BlockSpec((tm, S), lambda bb, n, i: (i, 0)),
                  pl.BlockSpec((S, tn), lambda bb, n, i: (bb, n)),
                  pl.BlockSpec((S, tn), lambda bb, n, i: (bb, n))],
        out_specs=pl.BlockSpec((tm, tn), lambda bb, n, i: (bb * ni + i, n)),
        out_shape=jax.ShapeDtypeStruct((M, N), BF16),
        compiler_params=_params("parallel", "parallel", "arbitrary"),
        name="dft_seq",
    )(cos_s, sin_s, a, b)


def _out_proj_kernel(o_ref, f_ref, wa_ref, wf_ref, g_ref, x_ref, y_ref):
    m = (jnp.dot(o_ref[...], wa_ref[...], preferred_element_type=F32)
         + jnp.dot(f_ref[...], wf_ref[...], preferred_element_type=F32))
    y_ref[...] = x_ref[...] + _rms(m, g_ref[...])


def _out_proj(o, fo, w_out, g, x, *, tm):
    M, D = x.shape
    ka, kf = o.shape[1], fo.shape[1]
    assert ka % kf == 0 and w_out.shape[0] == ka + kf
    return pl.pallas_call(
        _out_proj_kernel,
        grid=(M // tm,),
        in_specs=[pl.BlockSpec((tm, ka), lambda i: (i, 0)),
                  pl.BlockSpec((tm, kf), lambda i: (i, 0)),
                  pl.BlockSpec((ka, D), lambda i: (0, 0)),
                  pl.BlockSpec((kf, D), lambda i: (ka // kf, 0)),
                  pl.BlockSpec((1, D), lambda i: (0, 0)),
                  pl.BlockSpec((tm, D), lambda i: (i, 0))],
        out_specs=pl.BlockSpec((tm, D), lambda i: (i, 0)),
        out_shape=jax.ShapeDtypeStruct((M, D), F32),
        compiler_params=_params("parallel"),
        name="out_proj",
    )(o, fo, w_out, w_out, g.reshape(1, D), x)


def _tiles(seq_len):
    return dict(tm_block=min(512, seq_len), tf_block=512, tm_proj=min(1024, seq_len), tn_proj=1024,
                tm_out=min(512, seq_len), tm_dft=min(512, seq_len), tn_dft=512,
                attn=min(ATTN_TILE, seq_len))


def _run_trunk(x3, p, consts):
    B, S, D = x3.shape
    x = x3.reshape(B * S, D)
    t = _tiles(S)
    depth = p["norm_pre_mix"].shape[0]
    H = p["rel_bias"].shape[1]
    head_dim = p["ab_lambda_q1"].shape[1]
    attn_width = H * 2 * head_dim
    fgroup = (D - attn_width) // N_FGROUPS
    cos_c, sin_c = consts["dft_channel"]
    cs_c = jnp.concatenate([cos_c, sin_c], axis=1)
    cos_s, sin_s = consts["dft_seq"][S]
    for i in range(depth):
        j = i // 2
        if i % 2 == 0:
            lam_init = _lambda_init(i)
            z = _norm_matmul(x, p["norm_pre_mix"][i], p["ab_w_in"][j], tm=t["tm_proj"], tn=t["tn_proj"])
            lam = (consts["lam"][j, 0] + lam_init).reshape(1, 1)
            o = _diff_attention(z, lam, consts["bias_tiles"], p["ab_subln"][j], B=B, S=S, H=H,
                                head_dim=head_dim, out_scale=1.0 - lam_init, T=t["attn"])
            fa, fb = _dft_channel(z, 3 * attn_width, cs_c, tm=t["tm_proj"])
            fo = _dft_seq(fa, fb, cos_s, sin_s, B=B, S=S, tm=t["tm_dft"], tn=t["tn_dft"])
            x = _out_proj(o, fo, p["ab_w_out"][j], p["norm_post_mix"][i], x, tm=t["tm_out"])
        else:
            x = _gated_block(x, S, p["norm_pre_mix"][i], p["norm_post_mix"][i], [p["c_w_in"][j]],
                             p["c_conv"][j], None, p["c_w_out"][j], mode="conv",
                             tm=t["tm_block"], tf=t["tf_block"])
        x = _gated_block(x, S, p["norm_pre_ffn"][i], p["norm_post_ffn"][i],
                         [p["ffn_w_gate"][i], p["ffn_w_up"][i]], p["ffn_conv"][i], p["ffn_conv_b"][i],
                         p["ffn_w_down"][i], mode="ffn", tm=t["tm_block"], tf=t["tf_block"])
    return x.reshape(B, S, D)


def kernel(x_prompt, x_sample, rel_bias, norm_pre_mix, norm_post_mix, norm_pre_ffn, norm_post_ffn, ab_w_in, ab_w_out, ab_lambda_q1, ab_lambda_k1, ab_lambda_q2, ab_lambda_k2, ab_subln, c_w_in, c_conv, c_w_out, ffn_w_gate, ffn_w_up, ffn_conv, ffn_conv_b, ffn_w_down):
    p = dict(rel_bias=rel_bias, norm_pre_mix=norm_pre_mix, norm_post_mix=norm_post_mix,
             norm_pre_ffn=norm_pre_ffn, norm_post_ffn=norm_post_ffn,
             ab_w_in=ab_w_in.astype(BF16), ab_w_out=ab_w_out.astype(BF16),
             ab_lambda_q1=ab_lambda_q1, ab_subln=ab_subln,
             c_w_in=c_w_in.astype(BF16), c_conv=c_conv, c_w_out=c_w_out.astype(BF16),
             ffn_w_gate=ffn_w_gate.astype(BF16), ffn_w_up=ffn_w_up.astype(BF16),
             ffn_conv=ffn_conv, ffn_conv_b=ffn_conv_b, ffn_w_down=ffn_w_down.astype(BF16))
    H = rel_bias.shape[1]
    attn_width = H * 2 * ab_lambda_q1.shape[1]
    fgroup = (x_prompt.shape[-1] - attn_width) // N_FGROUPS
    bias_tiles, lam = _attn_consts(rel_bias, ab_lambda_q1, ab_lambda_k1, ab_lambda_q2, ab_lambda_k2,
                                   T=ATTN_TILE)
    consts = dict(bias_tiles=bias_tiles, lam=lam,
                  dft_channel=_dft_tables(fgroup, fgroup ** -0.5),
                  dft_seq={s: _dft_tables(s, s ** -0.5)
                           for s in sorted({x_prompt.shape[1], x_sample.shape[1]})})
    return (_run_trunk(x_prompt, p, consts), _run_trunk(x_sample, p, consts))
```

```python
import functools
import math

import jax
import jax.numpy as jnp
from jax import lax
from jax.experimental import pallas as pl
from jax.experimental.pallas import tpu as pltpu

F32 = jnp.float32
BF16 = jnp.bfloat16

EPS = 1e-6
MAX_DISTANCE = 128
N_FGROUPS = 4
V7X_VMEM_LIMIT_BYTES = 56 * 1024 * 1024
HALO_ROWS = 8
ATTN_TILE = 256
EXP_ROWS = 64
NEG_BIG = -1e30
LANES = 128
LOG2_E = math.log2(math.e)


def _params(*semantics):
    return pltpu.CompilerParams(dimension_semantics=semantics,
                                vmem_limit_bytes=V7X_VMEM_LIMIT_BYTES)


def _rms(x, g):
    ms = jnp.mean(x * x, axis=-1, keepdims=True)
    return (x * lax.rsqrt(ms + EPS)) * g


def _lambda_init(layer_idx):
    return 0.8 - 0.6 * math.exp(-0.3 * layer_idx)


def _conv_rows(t, cw, tm):
    n = t.shape[0]
    prev = pltpu.roll(t, 1, 0)[:tm]
    nxt = pltpu.roll(t, n - 1, 0)[:tm]
    return prev * cw[0:1] + t[:tm] * cw[1:2] + nxt * cw[2:3]


def _gated_block_kernel(xp_ref, x_ref, xn_ref, gpre_ref, *rest, mode, tm, tiles_per_seq):
    if mode == "ffn":
        wg_ref, wu_ref, cw_ref, cb_ref, wd_ref, gpost_ref, o_ref, h_scr, acc_scr = rest
    else:
        wb_ref, wc_ref, wx_ref, cw_ref, wd_ref, gpost_ref, o_ref, h_scr, acc_scr = rest
    i = pl.program_id(0)
    j = pl.program_id(1)
    nj = pl.num_programs(1)

    @pl.when(j == 0)
    def _():
        g = gpre_ref[...]
        h_scr[0:tm, :] = _rms(x_ref[...], g).astype(BF16)
        it = i % tiles_per_seq
        keep_prev = jnp.where(it != 0, 1.0, 0.0)
        keep_next = jnp.where(it != tiles_per_seq - 1, 1.0, 0.0)
        halo = jnp.concatenate([_rms(xn_ref[...], g) * keep_next,
                                _rms(xp_ref[...], g) * keep_prev], axis=0)
        h_scr[tm:tm + 2 * HALO_ROWS, :] = halo.astype(BF16)
        acc_scr[...] = jnp.zeros_like(acc_scr)

    h_all = h_scr[...]
    h_main = h_scr[0:tm, :]
    cw = cw_ref[...]
    if mode == "ffn":
        gate = jnp.dot(h_all, wg_ref[...], preferred_element_type=F32)
        up = jnp.dot(h_main, wu_ref[...], preferred_element_type=F32)
        g = _conv_rows(gate, cw, tm) + cb_ref[...]
        a = (g * jax.nn.sigmoid(g)) * up
    else:
        bg = jnp.dot(h_main, wb_ref[...], preferred_element_type=F32)
        cg = jnp.dot(h_all, wc_ref[...], preferred_element_type=F32)
        xv = jnp.dot(h_all, wx_ref[...], preferred_element_type=F32)
        a = bg * _conv_rows(cg * xv, cw, tm)
    acc_scr[...] += jnp.dot(a.astype(BF16), wd_ref[...], preferred_element_type=F32)

    @pl.when(j == nj - 1)
    def _():
        o_ref[...] = x_ref[...] + _rms(acc_scr[...], gpost_ref[...])


def _gated_block(x, seq_len, gpre, gpost, w_in_list, conv_w, conv_b, w_out, *, mode, tm, tf):
    M, D = x.shape
    C = w_out.shape[0]
    assert M % tm == 0 and seq_len % tm == 0 and C % tf == 0 and tm % (2 * HALO_ROWS) == 0
    ni, nj = M // tm, C // tf
    tiles_per_seq = seq_len // tm
    blocks_per_tile = tm // HALO_ROWS
    last_block = M // HALO_ROWS - 1

    row = lambda i, j: (i, 0)
    in_specs = [
        pl.BlockSpec((HALO_ROWS, D), lambda i, j: (jnp.maximum(i * blocks_per_tile - 1, 0), 0)),
        pl.BlockSpec((tm, D), row),
        pl.BlockSpec((HALO_ROWS, D), lambda i, j: (jnp.minimum((i + 1) * blocks_per_tile, last_block), 0)),
        pl.BlockSpec((1, D), lambda i, j: (0, 0)),
    ]
    args = [x, x, x, gpre.reshape(1, D)]
    if mode == "ffn":
        wg, wu = w_in_list
        in_specs += [pl.BlockSpec((D, tf), lambda i, j: (0, j)),
                     pl.BlockSpec((D, tf), lambda i, j: (0, j)),
                     pl.BlockSpec((3, tf), lambda i, j: (0, j)),
                     pl.BlockSpec((1, tf), lambda i, j: (0, j))]
        args += [wg, wu, conv_w, conv_b.reshape(1, C)]
    else:
        (w_in,) = w_in_list
        in_specs += [pl.BlockSpec((D, tf), lambda i, j: (0, j)),
                     pl.BlockSpec((D, tf), lambda i, j: (0, nj + j)),
                     pl.BlockSpec((D, tf), lambda i, j: (0, 2 * nj + j)),
                     pl.BlockSpec((3, tf), lambda i, j: (0, j))]
        args += [w_in, w_in, w_in, conv_w]
    in_specs += [pl.BlockSpec((tf, D), lambda i, j: (j, 0)),
                 pl.BlockSpec((1, D), lambda i, j: (0, 0))]
    args += [w_out, gpost.reshape(1, D)]

    return pl.pallas_call(
        functools.partial(_gated_block_kernel, mode=mode, tm=tm, tiles_per_seq=tiles_per_seq),
        grid=(ni, nj),
        in_specs=in_specs,
        out_specs=pl.BlockSpec((tm, D), row),
        out_shape=jax.ShapeDtypeStruct((M, D), F32),
        scratch_shapes=[pltpu.VMEM((tm + 2 * HALO_ROWS, D), BF16),
                        pltpu.VMEM((tm, D), F32)],
        compiler_params=_params("parallel", "arbitrary"),
        name="gated_block_" + mode,
    )(*args)


def _norm_matmul_kernel(x_ref, g_ref, w_ref, o_ref, h_scr):
    @pl.when(pl.program_id(1) == 0)
    def _():
        h_scr[...] = _rms(x_ref[...], g_ref[...]).astype(BF16)

    o_ref[...] = jnp.dot(h_scr[...], w_ref[...], preferred_element_type=F32).astype(o_ref.dtype)


def _norm_matmul(x, g, w, *, tm, tn):
    M, D = x.shape
    N = w.shape[1]
    assert M % tm == 0 and N % tn == 0
    return pl.pallas_call(
        _norm_matmul_kernel,
        grid=(M // tm, N // tn),
        in_specs=[pl.BlockSpec((tm, D), lambda i, j: (i, 0)),
                  pl.BlockSpec((1, D), lambda i, j: (0, 0)),
                  pl.BlockSpec((D, tn), lambda i, j: (0, j))],
        out_specs=pl.BlockSpec((tm, tn), lambda i, j: (i, j)),
        out_shape=jax.ShapeDtypeStruct((M, N), BF16),
        scratch_shapes=[pltpu.VMEM((tm, D), BF16)],
        compiler_params=_params("parallel", "arbitrary"),
        name="norm_in_proj",
    )(x, g.reshape(1, D), w)


def _bucket_saturation_distance(num_buckets):
    nb = num_buckets // 2
    max_exact = nb // 2
    n = max_exact
    while max_exact + int(math.log(n / max_exact) / math.log(MAX_DISTANCE / max_exact)
                          * (nb - max_exact)) < nb - 1:
        n += 1
    return n


def _attn_consts_kernel(rb_ref, lq1_ref, lk1_ref, lq2_ref, lk2_ref, bias_ref, lam_ref, *, T, num_buckets):
    h = pl.program_id(0)
    nb = num_buckets // 2
    max_exact = nb // 2
    row = lax.broadcasted_iota(jnp.int32, (T, T), 0)
    col = lax.broadcasted_iota(jnp.int32, (T, T), 1)
    for t, off in enumerate((-2, -1, 0, 1, 2)):
        rel = col - row + off * T
        ret = jnp.where(rel > 0, nb, 0)
        n = jnp.abs(rel)
        nf = jnp.maximum(n, 1).astype(F32)
        large = max_exact + (jnp.log(nf / max_exact) / math.log(MAX_DISTANCE / max_exact)
                             * (nb - max_exact)).astype(jnp.int32)
        large = jnp.minimum(large, nb - 1)
        bucket = ret + jnp.where(n < max_exact, n, large)
        bias = jnp.zeros((T, T), F32)
        for b in range(num_buckets):
            bias = jnp.where(bucket == b, rb_ref[b, h], bias)
        bias_ref[0, t] = bias * LOG2_E
    s1 = jnp.sum(lq1_ref[...] * lk1_ref[...], axis=-1, keepdims=True)
    s2 = jnp.sum(lq2_ref[...] * lk2_ref[...], axis=-1, keepdims=True)
    lam_ref[...] = jnp.broadcast_to(jnp.exp(s1) - jnp.exp(s2), lam_ref.shape)


def _attn_consts(rel_bias, lq1, lk1, lq2, lk2, *, T):
    num_buckets, H = rel_bias.shape
    n_ab = lq1.shape[0]
    assert T + 1 >= _bucket_saturation_distance(num_buckets)
    vec = pl.BlockSpec(lq1.shape, lambda h: (0, 0))
    return pl.pallas_call(
        functools.partial(_attn_consts_kernel, T=T, num_buckets=num_buckets),
        grid=(H,),
        in_specs=[pl.BlockSpec(memory_space=pltpu.SMEM), vec, vec, vec, vec],
        out_specs=[pl.BlockSpec((1, 5, T, T), lambda h: (h, 0, 0, 0)),
                   pl.BlockSpec((n_ab, 128), lambda h: (0, 0))],
        out_shape=[jax.ShapeDtypeStruct((H, 5, T, T), F32),
                   jax.ShapeDtypeStruct((n_ab, 128), F32)],
        compiler_params=_params("arbitrary"),
        name="attn_consts",
    )(rel_bias, lq1, lk1, lq2, lk2)


def _attn_kernel(lam_ref, q_ref, k_ref, v_ref, bias_ref, sub_ref, o_ref,
                 qm_scr, s_scr, e_scr, mx_scr, ls_scr, *, T, nk, head_dim, out_scale):
    qi = pl.program_id(2)
    q = q_ref[...].astype(F32) * (head_dim ** -0.5 * LOG2_E)
    lane = lax.broadcasted_iota(jnp.int32, q.shape, 1)
    qm_scr[0:T, :] = jnp.where(lane < head_dim, q, 0.0).astype(BF16)
    qm_scr[T:2 * T, :] = jnp.where(lane >= head_dim, q, 0.0).astype(BF16)
    mx_scr[...] = jnp.full_like(mx_scr, NEG_BIG)
    n_lane_chunks = T // LANES

    def scores(kj, carry):
        start = pl.multiple_of(kj * T, T)
        b = bias_ref[0, jnp.clip(kj - qi, -2, 2) + 2]
        s = lax.dot_general(qm_scr[...], k_ref[pl.ds(start, T), :], (((1,), (1,)), ((), ())),
                            preferred_element_type=F32)
        s = s + jnp.concatenate([b, b], axis=0)
        s_scr[:, pl.ds(start, T)] = s
        mx = mx_scr[...]
        for c in range(n_lane_chunks):
            mx = jnp.maximum(mx, s[:, c * LANES:(c + 1) * LANES])
        mx_scr[...] = mx
        return carry

    lax.fori_loop(0, nk, scores, 0, unroll=8)

    mx_scr[...] = jnp.broadcast_to(jnp.max(mx_scr[...], axis=-1, keepdims=True), mx_scr.shape)

    def exps(rg, carry):
        r0 = pl.multiple_of(rg * EXP_ROWS, EXP_ROWS)
        rows = pl.ds(r0, EXP_ROWS)
        mb = mx_scr[rows, :]
        ls = jnp.zeros((EXP_ROWS, LANES), F32)
        for c in range(nk * n_lane_chunks):
            cols = slice(c * LANES, (c + 1) * LANES)
            e = jnp.exp2(s_scr[rows, cols] - mb)
            ls = ls + e
            e_scr[rows, cols] = e.astype(BF16)
        ls_scr[rows, :] = ls
        return carry

    lax.fori_loop(0, 2 * T // EXP_ROWS, exps, 0)
    acc = jnp.dot(e_scr[...], v_ref[...], preferred_element_type=F32)
    lam = lam_ref[0, 0]
    l = jnp.sum(ls_scr[...], axis=-1, keepdims=True)
    o = acc[0:T] / l[0:T] - lam * (acc[T:2 * T] / l[T:2 * T])
    o_ref[...] = (_rms(o, sub_ref[...]) * out_scale).astype(o_ref.dtype)


def _diff_attention(z, lam, bias_tiles, subln, *, B, S, H, head_dim, out_scale, T):
    M = z.shape[0]
    vd = 2 * head_dim
    assert vd == 128 and S % T == 0
    nq = S // T
    return pl.pallas_call(
        functools.partial(_attn_kernel, T=T, nk=nq, head_dim=head_dim, out_scale=out_scale),
        grid=(B, H, nq),
        in_specs=[pl.BlockSpec(memory_space=pltpu.SMEM),
                  pl.BlockSpec((T, vd), lambda b, h, i: (b * nq + i, h)),
                  pl.BlockSpec((S, vd), lambda b, h, i: (b, H + h)),
                  pl.BlockSpec((S, vd), lambda b, h, i: (b, 2 * H + h)),
                  pl.BlockSpec((1, 5, T, T), lambda b, h, i: (h, 0, 0, 0)),
                  pl.BlockSpec((1, vd), lambda b, h, i: (0, 0))],
        out_specs=pl.BlockSpec((T, vd), lambda b, h, i: (b * nq + i, h)),
        out_shape=jax.ShapeDtypeStruct((M, H * vd), BF16),
        scratch_shapes=[pltpu.VMEM((2 * T, vd), BF16),
                        pltpu.VMEM((2 * T, S), F32),
                        pltpu.VMEM((2 * T, S), BF16),
                        pltpu.VMEM((2 * T, LANES), F32),
                        pltpu.VMEM((2 * T, LANES), F32)],
        compiler_params=_params("parallel", "parallel", "arbitrary"),
        name="diff_attention",
    )(lam, z, z, z, bias_tiles, subln.reshape(1, vd))


def _dft_tables(n, scale):
    idx = jnp.arange(n, dtype=jnp.int32)
    ang = ((idx[:, None] * idx[None, :]) % n).astype(F32) * (2.0 * math.pi / n)
    return (jnp.cos(ang) * scale).astype(BF16), (jnp.sin(ang) * scale).astype(BF16)


def _dft_channel_kernel(f_ref, cs_ref, a_ref, b_ref):
    c = f_ref.shape[1]
    y = jnp.dot(f_ref[...], cs_ref[...], preferred_element_type=F32)
    a_ref[...] = y[:, :c].astype(a_ref.dtype)
    b_ref[...] = y[:, c:].astype(b_ref.dtype)


def _dft_channel(z, col0, cs, *, tm):
    M = z.shape[0]
    c = cs.shape[0]
    blk0 = col0 // c
    out = jax.ShapeDtypeStruct((M, N_FGROUPS * c), BF16)
    return pl.pallas_call(
        _dft_channel_kernel,
        grid=(M // tm, N_FGROUPS),
        in_specs=[pl.BlockSpec((tm, c), lambda i, g: (i, blk0 + g)),
                  pl.BlockSpec((c, 2 * c), lambda i, g: (0, 0))],
        out_specs=[pl.BlockSpec((tm, c), lambda i, g: (i, g)),
                   pl.BlockSpec((tm, c), lambda i, g: (i, g))],
        out_shape=[out, out],
        compiler_params=_params("parallel", "parallel"),
        name="dft_channel",
    )(z, cs)


def _dft_seq_kernel(cs_ref, ss_ref, a_ref, b_ref, o_ref):
    y = (jnp.dot(cs_ref[...], a_ref[...], preferred_element_type=F32)
         - jnp.dot(ss_ref[...], b_ref[...], preferred_element_type=F32))
    o_ref[...] = y.astype(o_ref.dtype)


def _dft_seq(a, b, cos_s, sin_s, *, B, S, tm, tn):
    M, N = a.shape
    ni = S // tm
    return pl.pallas_call(
        _dft_seq_kernel,
        grid=(B, N // tn, ni),
        in_specs=[pl.BlockSpec((tm, S), lambda bb, n, i: (i, 0)),
                  pl.BlockSpec((tm, S), lambda bb, n, i: (i, 0)),
                  pl.BlockSpec((S, tn), lambda bb, n, i: (bb, n)),
                  pl.BlockSpec((S, tn), lambda bb, n, i: (bb, n))],
        out_specs=pl.BlockSpec((tm, tn), lambda bb, n, i: (bb * ni + i, n)),
        out_shape=jax.ShapeDtypeStruct((M, N), BF16),
        compiler_params=_params("parallel", "parallel", "arbitrary"),
        name="dft_seq",
    )(cos_s, sin_s, a, b)


def _out_proj_kernel(o_ref, f_ref, wa_ref, wf_ref, g_ref, x_ref, y_ref):
    m = (jnp.dot(o_ref[...], wa_ref[...], preferred_element_type=F32)
         + jnp.dot(f_ref[...], wf_ref[...], preferred_element_type=F32))
    y_ref[...] = x_ref[...] + _rms(m, g_ref[...])


def _out_proj(o, fo, w_out, g, x, *, tm):
    M, D = x.shape
    ka, kf = o.shape[1], fo.shape[1]
    assert ka % kf == 0 and w_out.shape[0] == ka + kf
    return pl.pallas_call(
        _out_proj_kernel,
        grid=(M // tm,),
        in_specs=[pl.BlockSpec((tm, ka), lambda i: (i, 0)),
                  pl.BlockSpec((tm, kf), lambda i: (i, 0)),
                  pl.BlockSpec((ka, D), lambda i: (0, 0)),
                  pl.BlockSpec((kf, D), lambda i: (ka // kf, 0)),
                  pl.BlockSpec((1, D), lambda i: (0, 0)),
                  pl.BlockSpec((tm, D), lambda i: (i, 0))],
        out_specs=pl.BlockSpec((tm, D), lambda i: (i, 0)),
        out_shape=jax.ShapeDtypeStruct((M, D), F32),
        compiler_params=_params("parallel"),
        name="out_proj",
    )(o, fo, w_out, w_out, g.reshape(1, D), x)


def _tiles(seq_len):
    return dict(tm_block=min(512, seq_len), tf_block=512, tm_proj=min(1024, seq_len), tn_proj=1024,
                tm_out=min(512, seq_len), tm_dft=min(512, seq_len), tn_dft=512,
                attn=min(ATTN_TILE, seq_len))


def _run_trunk(x3, p, consts):
    B, S, D = x3.shape
    x = x3.reshape(B * S, D)
    t = _tiles(S)
    depth = p["norm_pre_mix"].shape[0]
    H = p["rel_bias"].shape[1]
    head_dim = p["ab_lambda_q1"].shape[1]
    attn_width = H * 2 * head_dim
    fgroup = (D - attn_width) // N_FGROUPS
    cos_c, sin_c = consts["dft_channel"]
    cs_c = jnp.concatenate([cos_c, sin_c], axis=1)
    cos_s, sin_s = consts["dft_seq"][S]
    for i in range(depth):
        j = i // 2
        if i % 2 == 0:
            lam_init = _lambda_init(i)
            z = _norm_matmul(x, p["norm_pre_mix"][i], p["ab_w_in"][j], tm=t["tm_proj"], tn=t["tn_proj"])
            lam = (consts["lam"][j, 0] + lam_init).reshape(1, 1)
            o = _diff_attention(z, lam, consts["bias_tiles"], p["ab_subln"][j], B=B, S=S, H=H,
                                head_dim=head_dim, out_scale=1.0 - lam_init, T=t["attn"])
            fa, fb = _dft_channel(z, 3 * attn_width, cs_c, tm=t["tm_proj"])
            fo = _dft_seq(fa, fb, cos_s, sin_s, B=B, S=S, tm=t["tm_dft"], tn=t["tn_dft"])
            x = _out_proj(o, fo, p["ab_w_out"][j], p["norm_post_mix"][i], x, tm=t["tm_out"])
        else:
            x = _gated_block(x, S, p["norm_pre_mix"][i], p["norm_post_mix"][i], [p["c_w_in"][j]],
                             p["c_conv"][j], None, p["c_w_out"][j], mode="conv",
                             tm=t["tm_block"], tf=t["tf_block"])
        x = _gated_block(x, S, p["norm_pre_ffn"][i], p["norm_post_ffn"][i],
                         [p["ffn_w_gate"][i], p["ffn_w_up"][i]], p["ffn_conv"][i], p["ffn_conv_b"][i],
                         p["ffn_w_down"][i], mode="ffn", tm=t["tm_block"], tf=t["tf_block"])
    return x.reshape(B, S, D)


def kernel(x_prompt, x_sample, rel_bias, norm_pre_mix, norm_post_mix, norm_pre_ffn, norm_post_ffn, ab_w_in, ab_w_out, ab_lambda_q1, ab_lambda_k1, ab_lambda_q2, ab_lambda_k2, ab_subln, c_w_in, c_conv, c_w_out, ffn_w_gate, ffn_w_up, ffn_conv, ffn_conv_b, ffn_w_down):
    p = dict(rel_bias=rel_bias, norm_pre_mix=norm_pre_mix, norm_post_mix=norm_post_mix,
             norm_pre_ffn=norm_pre_ffn, norm_post_ffn=norm_post_ffn,
             ab_w_in=ab_w_in.astype(BF16), ab_w_out=ab_w_out.astype(BF16),
             ab_lambda_q1=ab_lambda_q1, ab_subln=ab_subln,
             c_w_in=c_w_in.astype(BF16), c_conv=c_conv, c_w_out=c_w_out.astype(BF16),
             ffn_w_gate=ffn_w_gate.astype(BF16), ffn_w_up=ffn_w_up.astype(BF16),
             ffn_conv=ffn_conv, ffn_conv_b=ffn_conv_b, ffn_w_down=ffn_w_down.astype(BF16))
    H = rel_bias.shape[1]
    attn_width = H * 2 * ab_lambda_q1.shape[1]
    fgroup = (x_prompt.shape[-1] - attn_width) // N_FGROUPS
    bias_tiles, lam = _attn_consts(rel_bias, ab_lambda_q1, ab_lambda_k1, ab_lambda_q2, ab_lambda_k2,
                                   T=ATTN_TILE)
    consts = dict(bias_tiles=bias_tiles, lam=lam,
                  dft_channel=_dft_tables(fgroup, fgroup ** -0.5),
                  dft_seq={s: _dft_tables(s, s ** -0.5)
                           for s in sorted({x_prompt.shape[1], x_sample.shape[1]})})
    return (_run_trunk(x_prompt, p, consts), _run_trunk(x_sample, p, consts))
```

```python
import functools
import math

import jax
import jax.numpy as jnp
from jax import lax
from jax.experimental import pallas as pl
from jax.experimental.pallas import tpu as pltpu

F32 = jnp.float32
BF16 = jnp.bfloat16

EPS = 1e-6
MAX_DISTANCE = 128
N_FGROUPS = 4
V7X_VMEM_LIMIT_BYTES = 56 * 1024 * 1024
HALO_ROWS = 8
ATTN_TILE = 256
ATTN_KEY_BLOCK = 512
NEG_BIG = -1e30
LANES = 128
BF16_SUBLANES = 16
LOG2_E = math.log2(math.e)


def _params(*semantics):
    return pltpu.CompilerParams(dimension_semantics=semantics,
                                vmem_limit_bytes=V7X_VMEM_LIMIT_BYTES)


def _rms(x, g):
    ms = jnp.mean(x * x, axis=-1, keepdims=True)
    return (x * lax.rsqrt(ms + EPS)) * g


def _lambda_init(layer_idx):
    return 0.8 - 0.6 * math.exp(-0.3 * layer_idx)


def _conv_rows(t, cw, tm):
    n = t.shape[0]
    prev = pltpu.roll(t, 1, 0)[:tm]
    nxt = pltpu.roll(t, n - 1, 0)[:tm]
    return prev * cw[0:1] + t[:tm] * cw[1:2] + nxt * cw[2:3]


def _gated_block_kernel(xp_ref, x_ref, xn_ref, gpre_ref, *rest, mode, tm, tiles_per_seq):
    if mode == "ffn":
        wg_ref, wu_ref, cw_ref, cb_ref, wd_ref, gpost_ref, o_ref, h_scr, acc_scr = rest
    else:
        wb_ref, wc_ref, wx_ref, cw_ref, wd_ref, gpost_ref, o_ref, h_scr, acc_scr = rest
    i = pl.program_id(0)
    j = pl.program_id(1)
    nj = pl.num_programs(1)

    @pl.when(j == 0)
    def _():
        g = gpre_ref[...]
        h_scr[0:tm, :] = _rms(x_ref[...], g).astype(BF16)
        it = i % tiles_per_seq
        keep_prev = jnp.where(it != 0, 1.0, 0.0)
        keep_next = jnp.where(it != tiles_per_seq - 1, 1.0, 0.0)
        halo = jnp.concatenate([_rms(xn_ref[...], g) * keep_next,
                                _rms(xp_ref[...], g) * keep_prev], axis=0)
        h_scr[tm:tm + 2 * HALO_ROWS, :] = halo.astype(BF16)
        acc_scr[...] = jnp.zeros_like(acc_scr)

    h_all = h_scr[...]
    h_main = h_scr[0:tm, :]
    cw = cw_ref[...]
    if mode == "ffn":
        gate = jnp.dot(h_all, wg_ref[...], preferred_element_type=F32)
        up = jnp.dot(h_main, wu_ref[...], preferred_element_type=F32)
        g = _conv_rows(gate, cw, tm) + cb_ref[...]
        a = (g * jax.nn.sigmoid(g)) * up
    else:
        bg = jnp.dot(h_main, wb_ref[...], preferred_element_type=F32)
        cg = jnp.dot(h_all, wc_ref[...], preferred_element_type=F32)
        xv = jnp.dot(h_all, wx_ref[...], preferred_element_type=F32)
        a = bg * _conv_rows(cg * xv, cw, tm)
    acc_scr[...] += jnp.dot(a.astype(BF16), wd_ref[...], preferred_element_type=F32)

    @pl.when(j == nj - 1)
    def _():
        o_ref[...] = x_ref[...] + _rms(acc_scr[...], gpost_ref[...])


def _gated_block(x, seq_len, gpre, gpost, w_in_list, conv_w, conv_b, w_out, layer, *, mode, tm, tf):
    M, D = x.shape
    C = w_out.shape[1]
    assert M % tm == 0 and seq_len % tm == 0 and C % tf == 0 and tm % (2 * HALO_ROWS) == 0
    ni, nj = M // tm, C // tf
    tiles_per_seq = seq_len // tm
    blocks_per_tile = tm // HALO_ROWS
    last_block = M // HALO_ROWS - 1

    row = lambda i, j: (i, 0)
    in_specs = [
        pl.BlockSpec((HALO_ROWS, D), lambda i, j: (jnp.maximum(i * blocks_per_tile - 1, 0), 0)),
        pl.BlockSpec((tm, D), row),
        pl.BlockSpec((HALO_ROWS, D), lambda i, j: (jnp.minimum((i + 1) * blocks_per_tile, last_block), 0)),
        pl.BlockSpec((1, D), lambda i, j: (0, 0)),
    ]
    args = [x, x, x, gpre.reshape(1, D)]
    if mode == "ffn":
        wg, wu = w_in_list
        in_specs += [pl.BlockSpec((None, D, tf), lambda i, j: (layer, 0, j)),
                     pl.BlockSpec((None, D, tf), lambda i, j: (layer, 0, j)),
                     pl.BlockSpec((3, tf), lambda i, j: (0, j)),
                     pl.BlockSpec((1, tf), lambda i, j: (0, j))]
        args += [wg, wu, conv_w, conv_b.reshape(1, C)]
    else:
        (w_in,) = w_in_list
        in_specs += [pl.BlockSpec((None, D, tf), lambda i, j: (layer, 0, j)),
                     pl.BlockSpec((None, D, tf), lambda i, j: (layer, 0, nj + j)),
                     pl.BlockSpec((None, D, tf), lambda i, j: (layer, 0, 2 * nj + j)),
                     pl.BlockSpec((3, tf), lambda i, j: (0, j))]
        args += [w_in, w_in, w_in, conv_w]
    in_specs += [pl.BlockSpec((None, tf, D), lambda i, j: (layer, j, 0)),
                 pl.BlockSpec((1, D), lambda i, j: (0, 0))]
    args += [w_out, gpost.reshape(1, D)]

    return pl.pallas_call(
        functools.partial(_gated_block_kernel, mode=mode, tm=tm, tiles_per_seq=tiles_per_seq),
        grid=(ni, nj),
        in_specs=in_specs,
        out_specs=pl.BlockSpec((tm, D), row),
        out_shape=jax.ShapeDtypeStruct((M, D), F32),
        scratch_shapes=[pltpu.VMEM((tm + 2 * HALO_ROWS, D), BF16),
                        pltpu.VMEM((tm, D), F32)],
        compiler_params=_params("parallel", "arbitrary"),
        name="gated_block_" + mode,
    )(*args)


def _norm_in_proj_kernel(x_ref, g_ref, w_ref, qkv_ref, f_ref, h_scr, *, n_qkv_tiles):
    j = pl.program_id(1)

    @pl.when(j == 0)
    def _():
        h_scr[...] = _rms(x_ref[...], g_ref[...]).astype(BF16)

    y = jnp.dot(h_scr[...], w_ref[...], preferred_element_type=F32)

    @pl.when(j < n_qkv_tiles)
    def _():
        for c in range(qkv_ref.shape[0]):
            qkv_ref[c] = y[:, c * LANES:(c + 1) * LANES].astype(qkv_ref.dtype)

    @pl.when(j >= n_qkv_tiles)
    def _():
        f_ref[...] = y.astype(f_ref.dtype)


def _norm_in_proj(x, g, w, layer, qkv_width, *, tm, tn):
    M, D = x.shape
    N = w.shape[2]
    assert M % tm == 0 and qkv_width % tn == 0 and (N - qkv_width) % tn == 0 and tn % LANES == 0
    n_qkv_tiles = qkv_width // tn
    cols = tn // LANES
    return pl.pallas_call(
        functools.partial(_norm_in_proj_kernel, n_qkv_tiles=n_qkv_tiles),
        grid=(M // tm, N // tn),
        in_specs=[pl.BlockSpec((tm, D), lambda i, j: (i, 0)),
                  pl.BlockSpec((1, D), lambda i, j: (0, 0)),
                  pl.BlockSpec((None, D, tn), lambda i, j: (layer, 0, j))],
        out_specs=[pl.BlockSpec((cols, tm, LANES), lambda i, j: (jnp.minimum(j, n_qkv_tiles - 1), i, 0)),
                   pl.BlockSpec((tm, tn), lambda i, j: (i, jnp.maximum(j - n_qkv_tiles, 0)))],
        out_shape=[jax.ShapeDtypeStruct((qkv_width // LANES, M, LANES), BF16),
                   jax.ShapeDtypeStruct((M, N - qkv_width), BF16)],
        scratch_shapes=[pltpu.VMEM((tm, D), BF16)],
        compiler_params=_params("parallel", "arbitrary"),
        name="norm_in_proj",
    )(x, g.reshape(1, D), w)


def _bucket_saturation_distance(num_buckets):
    nb = num_buckets // 2
    max_exact = nb // 2
    n = max_exact
    while max_exact + int(math.log(n / max_exact) / math.log(MAX_DISTANCE / max_exact)
                          * (nb - max_exact)) < nb - 1:
        n += 1
    return n


def _attn_consts_kernel(rb_ref, lq1_ref, lk1_ref, lq2_ref, lk2_ref, bias_ref, lam_ref, *, T, num_buckets):
    h = pl.program_id(0)
    nb = num_buckets // 2
    max_exact = nb // 2
    row = lax.broadcasted_iota(jnp.int32, (T, T), 0)
    col = lax.broadcasted_iota(jnp.int32, (T, T), 1)
    for t, off in enumerate((-2, -1, 0, 1, 2)):
        rel = row - col + off * T
        ret = jnp.where(rel > 0, nb, 0)
        n = jnp.abs(rel)
        nf = jnp.maximum(n, 1).astype(F32)
        large = max_exact + (jnp.log(nf / max_exact) / math.log(MAX_DISTANCE / max_exact)
                             * (nb - max_exact)).astype(jnp.int32)
        large = jnp.minimum(large, nb - 1)
        bucket = ret + jnp.where(n < max_exact, n, large)
        bias = jnp.zeros((T, T), F32)
        for b in range(num_buckets):
            bias = jnp.where(bucket == b, rb_ref[b, h], bias)
        bias_ref[0, t] = bias * LOG2_E
    s1 = jnp.sum(lq1_ref[...] * lk1_ref[...], axis=-1, keepdims=True)
    s2 = jnp.sum(lq2_ref[...] * lk2_ref[...], axis=-1, keepdims=True)
    lam_ref[...] = jnp.broadcast_to(jnp.exp(s1) - jnp.exp(s2), lam_ref.shape)


def _attn_consts(rel_bias, lq1, lk1, lq2, lk2, *, T):
    num_buckets, H = rel_bias.shape
    n_ab = lq1.shape[0]
    assert T + 1 >= _bucket_saturation_distance(num_buckets)
    vec = pl.BlockSpec(lq1.shape, lambda h: (0, 0))
    return pl.pallas_call(
        functools.partial(_attn_consts_kernel, T=T, num_buckets=num_buckets),
        grid=(H,),
        in_specs=[pl.BlockSpec(memory_space=pltpu.SMEM), vec, vec, vec, vec],
        out_specs=[pl.BlockSpec((1, 5, T, T), lambda h: (h, 0, 0, 0)),
                   pl.BlockSpec((n_ab, 128), lambda h: (0, 0))],
        out_shape=[jax.ShapeDtypeStruct((H, 5, T, T), F32),
                   jax.ShapeDtypeStruct((n_ab, 128), F32)],
        compiler_params=_params("arbitrary"),
        name="attn_consts",
    )(rel_bias, lq1, lk1, lq2, lk2)


def _attn_kernel(lam_ref, q_ref, k_ref, v_ref, bias_ref, sub_ref, o_ref,
                 qm_scr, vt_scr, *, T, nk, head_dim, out_scale):
    qi = pl.program_id(2)
    vd = v_ref.shape[1]

    @pl.when(qi == 0)
    def _():
        vt_scr[0:vd, :] = v_ref[...].T
        vt_scr[vd:, :] = jnp.ones((vt_scr.shape[0] - vd, vt_scr.shape[1]), BF16)

    q = q_ref[...].astype(F32) * (head_dim ** -0.5 * LOG2_E)
    lane = lax.broadcasted_iota(jnp.int32, q.shape, 1)
    qm_scr[0:T, :] = jnp.where(lane < head_dim, q, 0.0).astype(BF16)
    qm_scr[T:2 * T, :] = jnp.where(lane >= head_dim, q, 0.0).astype(BF16)
    qm = qm_scr[...]
    key_block = min(ATTN_KEY_BLOCK, nk * T)
    tiles_per_block = key_block // T
    n_blocks = nk // tiles_per_block

    def keys_of(j):
        return slice(j * key_block, (j + 1) * key_block)

    def scores(j):
        bias = jnp.concatenate([bias_ref[0, jnp.clip(j * tiles_per_block + c - qi, -2, 2) + 2]
                                for c in range(tiles_per_block)], axis=0)
        return lax.dot_general(k_ref[keys_of(j), :], qm, (((1,), (1,)), ((), ())),
                               preferred_element_type=F32) + jnp.concatenate([bias, bias], axis=1)

    def weighted_values(j, e):
        return jnp.dot(vt_scr[:, keys_of(j)], e, preferred_element_type=F32)

    m = jnp.full((1, 2 * T), NEG_BIG, F32)
    acc = jnp.zeros((vt_scr.shape[0], 2 * T), F32)
    t_next = scores(0)
    e_prev = alpha_prev = None
    for j in range(n_blocks):
        t = t_next
        if j + 1 < n_blocks:
            t_next = scores(j + 1)
        m_new = jnp.maximum(m, jnp.max(t, axis=0, keepdims=True))
        alpha = jnp.exp2(m - m_new)
        e = jnp.exp2(t - m_new).astype(BF16)
        if j > 0:
            acc = alpha_prev * acc + weighted_values(j - 1, e_prev)
        e_prev, alpha_prev, m = e, alpha, m_new
    acc = alpha_prev * acc + weighted_values(n_blocks - 1, e_prev)
    ot = acc[0:vd] / acc[vd:vd + 1]
    lam = lam_ref[0, 0]
    o = (ot[:, 0:T] - lam * ot[:, T:2 * T]).T
    o_ref[...] = (_rms(o, sub_ref[...]) * out_scale).astype(o_ref.dtype)


def _diff_attention(qkv, lam, bias_tiles, subln, *, B, S, H, head_dim, out_scale, T):
    M = qkv.shape[1]
    vd = 2 * head_dim
    assert vd == LANES and S % T == 0 and qkv.shape[0] == 3 * H
    nq = S // T
    return pl.pallas_call(
        functools.partial(_attn_kernel, T=T, nk=nq, head_dim=head_dim, out_scale=out_scale),
        grid=(B, H, nq),
        in_specs=[pl.BlockSpec(memory_space=pltpu.SMEM),
                  pl.BlockSpec((None, T, vd), lambda b, h, i: (h, b * nq + i, 0)),
                  pl.BlockSpec((None, S, vd), lambda b, h, i: (H + h, b, 0)),
                  pl.BlockSpec((None, S, vd), lambda b, h, i: (2 * H + h, b, 0)),
                  pl.BlockSpec((1, 5, T, T), lambda b, h, i: (h, 0, 0, 0)),
                  pl.BlockSpec((1, vd), lambda b, h, i: (0, 0))],
        out_specs=pl.BlockSpec((T, vd), lambda b, h, i: (b * nq + i, h)),
        out_shape=jax.ShapeDtypeStruct((M, H * vd), BF16),
        scratch_shapes=[pltpu.VMEM((2 * T, vd), BF16),
                        pltpu.VMEM((vd + BF16_SUBLANES, S), BF16)],
        compiler_params=_params("parallel", "parallel", "arbitrary"),
        name="diff_attention",
    )(lam, qkv, qkv, qkv, bias_tiles, subln.reshape(1, vd))


def _dft_tables(n, scale):
    j = jnp.arange(n, dtype=jnp.int32)[:, None]

    def trig(k):
        ang = ((j * k[None, :]) % n).astype(F32) * (2.0 * math.pi / n)
        return jnp.cos(ang), jnp.sin(ang)

    w = min(LANES, n)
    c0, s0 = trig(jnp.arange(w, dtype=jnp.int32))
    c1, s1 = trig(jnp.arange(n // w, dtype=jnp.int32) * w)
    c0, s0, c1, s1 = c0[:, None, :], s0[:, None, :], c1[:, :, None], s1[:, :, None]
    cos = (c1 * c0 - s1 * s0) * scale
    sin = (s1 * c0 + c1 * s0) * scale
    return cos.reshape(n, n).astype(BF16), sin.reshape(n, n).astype(BF16)


def _dft_channel_kernel(f_ref, cs_ref, a_ref, b_ref):
    c = f_ref.shape[1]
    y = jnp.dot(f_ref[...], cs_ref[...], preferred_element_type=F32)
    a_ref[...] = y[:, :c].astype(a_ref.dtype)
    b_ref[...] = y[:, c:].astype(b_ref.dtype)


def _dft_channel(f, cs, *, tm):
    M = f.shape[0]
    c = cs.shape[0]
    assert f.shape[1] == N_FGROUPS * c
    out = jax.ShapeDtypeStruct((M, N_FGROUPS * c), BF16)
    return pl.pallas_call(
        _dft_channel_kernel,
        grid=(M // tm, N_FGROUPS),
        in_specs=[pl.BlockSpec((tm, c), lambda i, g: (i, g)),
                  pl.BlockSpec((c, 2 * c), lambda i, g: (0, 0))],
        out_specs=[pl.BlockSpec((tm, c), lambda i, g: (i, g)),
                   pl.BlockSpec((tm, c), lambda i, g: (i, g))],
        out_shape=[out, out],
        compiler_params=_params("parallel", "parallel"),
        name="dft_channel",
    )(f, cs)


def _dft_seq_kernel(cs_ref, ss_ref, a_ref, b_ref, o_ref):
    y = (jnp.dot(cs_ref[...], a_ref[...], preferred_element_type=F32)
         - jnp.dot(ss_ref[...], b_ref[...], preferred_element_type=F32))
    o_ref[...] = y.astype(o_ref.dtype)


def _dft_seq(a, b, cos_s, sin_s, *, B, S, tm, tn):
    M, N = a.shape
    ni = S // tm
    return pl.pallas_call(
        _dft_seq_kernel,
        grid=(B, N // tn, ni),
        in_specs=[pl.BlockSpec((tm, S), lambda bb, n, i: (i, 0)),
                  pl.BlockSpec((tm, S), lambda bb, n, i: (i, 0)),
                  pl.BlockSpec((S, tn), lambda bb, n, i: (bb, n)),
                  pl.BlockSpec((S, tn), lambda bb, n, i: (bb, n))],
        out_specs=pl.BlockSpec((tm, tn), lambda bb, n, i: (bb * ni + i, n)),
        out_shape=jax.ShapeDtypeStruct((M, N), BF16),
        compiler_params=_params("parallel", "parallel", "arbitrary"),
        name="dft_seq",
    )(cos_s, sin_s, a, b)


def _out_proj_kernel(o_ref, f_ref, wa_ref, wf_ref, g_ref, x_ref, y_ref):
    m = (jnp.dot(o_ref[...], wa_ref[...], preferred_element_type=F32)
         + jnp.dot(f_ref[...], wf_ref[...], preferred_element_type=F32))
    y_ref[...] = x_ref[...] + _rms(m, g_ref[...])


def _out_proj(o, fo, w_out, layer, g, x, *, tm):
    M, D = x.shape
    ka, kf = o.shape[1], fo.shape[1]
    assert ka % kf == 0 and w_out.shape[1] == ka + kf
    return pl.pallas_call(
        _out_proj_kernel,
        grid=(M // tm,),
        in_specs=[pl.BlockSpec((tm, ka), lambda i: (i, 0)),
                  pl.BlockSpec((tm, kf), lambda i: (i, 0)),
                  pl.BlockSpec((None, ka, D), lambda i: (layer, 0, 0)),
                  pl.BlockSpec((None, kf, D), lambda i: (layer, ka // kf, 0)),
                  pl.BlockSpec((1, D), lambda i: (0, 0)),
                  pl.BlockSpec((tm, D), lambda i: (i, 0))],
        out_specs=pl.BlockSpec((tm, D), lambda i: (i, 0)),
        out_shape=jax.ShapeDtypeStruct((M, D), F32),
        compiler_params=_params("parallel"),
        name="out_proj",
    )(o, fo, w_out, w_out, g.reshape(1, D), x)


def _tiles(seq_len):
    return dict(tm_block=min(512, seq_len), tf_block=512, tm_proj=min(1024, seq_len), tn_proj=1024,
                tm_out=min(512, seq_len), tm_dft=min(512, seq_len), tn_dft=512,
                attn=min(ATTN_TILE, seq_len))


def _run_trunk(x3, p, consts):
    B, S, D = x3.shape
    x = x3.reshape(B * S, D)
    t = _tiles(S)
    depth = p["norm_pre_mix"].shape[0]
    H = p["rel_bias"].shape[1]
    head_dim = p["ab_lambda_q1"].shape[1]
    attn_width = H * 2 * head_dim
    fgroup = (D - attn_width) // N_FGROUPS
    cos_c, sin_c = consts["dft_channel"]
    cs_c = jnp.concatenate([cos_c, sin_c], axis=1)
    cos_s, sin_s = consts["dft_seq"][S]
    for i in range(depth):
        j = i // 2
        if i % 2 == 0:
            lam_init = _lambda_init(i)
            qkv, f = _norm_in_proj(x, p["norm_pre_mix"][i], p["ab_w_in"], j, 3 * attn_width,
                                   tm=t["tm_proj"], tn=t["tn_proj"])
            lam = (consts["lam"][j, 0] + lam_init).reshape(1, 1)
            o = _diff_attention(qkv, lam, consts["bias_tiles"], p["ab_subln"][j], B=B, S=S, H=H,
                                head_dim=head_dim, out_scale=1.0 - lam_init, T=t["attn"])
            fa, fb = _dft_channel(f, cs_c, tm=t["tm_proj"])
            fo = _dft_seq(fa, fb, cos_s, sin_s, B=B, S=S, tm=t["tm_dft"], tn=t["tn_dft"])
            x = _out_proj(o, fo, p["ab_w_out"], j, p["norm_post_mix"][i], x, tm=t["tm_out"])
        else:
            x = _gated_block(x, S, p["norm_pre_mix"][i], p["norm_post_mix"][i], [p["c_w_in"]],
                             p["c_conv"][j], None, p["c_w_out"], j, mode="conv",
                             tm=t["tm_block"], tf=t["tf_block"])
        x = _gated_block(x, S, p["norm_pre_ffn"][i], p["norm_post_ffn"][i],
                         [p["ffn_w_gate"], p["ffn_w_up"]], p["ffn_conv"][i], p["ffn_conv_b"][i],
                         p["ffn_w_down"], i, mode="ffn", tm=t["tm_block"], tf=t["tf_block"])
    return x.reshape(B, S, D)


def kernel(x_prompt, x_sample, rel_bias, norm_pre_mix, norm_post_mix, norm_pre_ffn, norm_post_ffn, ab_w_in, ab_w_out, ab_lambda_q1, ab_lambda_k1, ab_lambda_q2, ab_lambda_k2, ab_subln, c_w_in, c_conv, c_w_out, ffn_w_gate, ffn_w_up, ffn_conv, ffn_conv_b, ffn_w_down):
    p = dict(rel_bias=rel_bias, norm_pre_mix=norm_pre_mix, norm_post_mix=norm_post_mix,
             norm_pre_ffn=norm_pre_ffn, norm_post_ffn=norm_post_ffn,
             ab_w_in=ab_w_in.astype(BF16), ab_w_out=ab_w_out.astype(BF16),
             ab_lambda_q1=ab_lambda_q1, ab_subln=ab_subln,
             c_w_in=c_w_in.astype(BF16), c_conv=c_conv, c_w_out=c_w_out.astype(BF16),
             ffn_w_gate=ffn_w_gate.astype(BF16), ffn_w_up=ffn_w_up.astype(BF16),
             ffn_conv=ffn_conv, ffn_conv_b=ffn_conv_b, ffn_w_down=ffn_w_down.astype(BF16))
    H = rel_bias.shape[1]
    attn_width = H * 2 * ab_lambda_q1.shape[1]
    fgroup = (x_prompt.shape[-1] - attn_width) // N_FGROUPS
    bias_tiles, lam = _attn_consts(rel_bias, ab_lambda_q1, ab_lambda_k1, ab_lambda_q2, ab_lambda_k2,
                                   T=ATTN_TILE)
    consts = dict(bias_tiles=bias_tiles, lam=lam,
                  dft_channel=_dft_tables(fgroup, fgroup ** -0.5),
                  dft_seq={s: _dft_tables(s, s ** -0.5)
                           for s in sorted({x_prompt.shape[1], x_sample.shape[1]})})
    return (_run_trunk(x_prompt, p, consts), _run_trunk(x_sample, p, consts))
```

```python
import functools
import math

import jax
import jax.numpy as jnp
from jax import lax
from jax.experimental import pallas as pl
from jax.experimental.pallas import tpu as pltpu

F32 = jnp.float32
BF16 = jnp.bfloat16

EPS = 1e-6
MAX_DISTANCE = 128
N_FGROUPS = 4
V7X_VMEM_LIMIT_BYTES = 60000 * 1024
HALO_ROWS = 8
ATTN_TILE = 256
ATTN_KEY_BLOCK = 512
NEG_BIG = -1e30
LANES = 128
BF16_SUBLANES = 16
LOG2_E = math.log2(math.e)


def _params(*semantics):
    return pltpu.CompilerParams(dimension_semantics=semantics,
                                vmem_limit_bytes=V7X_VMEM_LIMIT_BYTES)


def _rms(x, g):
    ms = jnp.mean(x * x, axis=-1, keepdims=True)
    return (x * lax.rsqrt(ms + EPS)) * g


def _lambda_init(layer_idx):
    return 0.8 - 0.6 * math.exp(-0.3 * layer_idx)


def _conv_rows(t, cw, tm):
    n = t.shape[0]
    prev = pltpu.roll(t, 1, 0)[:tm]
    nxt = pltpu.roll(t, n - 1, 0)[:tm]
    return prev * cw[0:1] + t[:tm] * cw[1:2] + nxt * cw[2:3]


def _gated_block_kernel(xp_ref, x_ref, xn_ref, gpre_ref, *rest, mode, tm, tiles_per_seq):
    if mode == "ffn":
        wg_ref, wu_ref, cw_ref, cb_ref, wd_ref, gpost_ref, o_ref, h_scr = rest
    else:
        wb_ref, wc_ref, wx_ref, cw_ref, wd_ref, gpost_ref, o_ref, h_scr = rest
    i = pl.program_id(0)
    j = pl.program_id(1)
    nj = pl.num_programs(1)

    @pl.when(j == 0)
    def _():
        g = gpre_ref[...]
        h_scr[0:tm, :] = _rms(x_ref[...], g).astype(BF16)
        it = i % tiles_per_seq
        keep_prev = jnp.where(it != 0, 1.0, 0.0)
        keep_next = jnp.where(it != tiles_per_seq - 1, 1.0, 0.0)
        halo = jnp.concatenate([_rms(xn_ref[...], g) * keep_next,
                                _rms(xp_ref[...], g) * keep_prev], axis=0)
        h_scr[tm:tm + 2 * HALO_ROWS, :] = halo.astype(BF16)
        o_ref[...] = jnp.zeros_like(o_ref)

    h_all = h_scr[...]
    h_main = h_scr[0:tm, :]
    cw = cw_ref[...]
    if mode == "ffn":
        gate = jnp.dot(h_all, wg_ref[...], preferred_element_type=F32)
        up = jnp.dot(h_main, wu_ref[...], preferred_element_type=F32)
        g = _conv_rows(gate, cw, tm) + cb_ref[...]
        a = (g * jax.nn.sigmoid(g)) * up
    else:
        bg = jnp.dot(h_main, wb_ref[...], preferred_element_type=F32)
        cg = jnp.dot(h_all, wc_ref[...], preferred_element_type=F32)
        xv = jnp.dot(h_all, wx_ref[...], preferred_element_type=F32)
        a = bg * _conv_rows(cg * xv, cw, tm)
    o_ref[...] += jnp.dot(a.astype(BF16), wd_ref[...], preferred_element_type=F32)

    @pl.when(j == nj - 1)
    def _():
        o_ref[...] = x_ref[...] + _rms(o_ref[...], gpost_ref[...])


def _gated_block(x, seq_len, gpre, gpost, w_in_list, conv_w, conv_b, w_out, layer, *, mode, tm, tf, out_buffers):
    M, D = x.shape
    C = w_out.shape[1]
    assert M % tm == 0 and seq_len % tm == 0 and C % tf == 0 and tm % (2 * HALO_ROWS) == 0
    ni, nj = M // tm, C // tf
    tiles_per_seq = seq_len // tm
    blocks_per_tile = tm // HALO_ROWS
    last_block = M // HALO_ROWS - 1

    row = lambda i, j: (i, 0)
    in_specs = [
        pl.BlockSpec((HALO_ROWS, D), lambda i, j: (jnp.maximum(i * blocks_per_tile - 1, 0), 0)),
        pl.BlockSpec((tm, D), row),
        pl.BlockSpec((HALO_ROWS, D), lambda i, j: (jnp.minimum((i + 1) * blocks_per_tile, last_block), 0)),
        pl.BlockSpec((1, D), lambda i, j: (0, 0)),
    ]
    args = [x, x, x, gpre.reshape(1, D)]
    if mode == "ffn":
        wg, wu = w_in_list
        in_specs += [pl.BlockSpec((None, D, tf), lambda i, j: (layer, 0, j)),
                     pl.BlockSpec((None, D, tf), lambda i, j: (layer, 0, j)),
                     pl.BlockSpec((3, tf), lambda i, j: (0, j)),
                     pl.BlockSpec((1, tf), lambda i, j: (0, j))]
        args += [wg, wu, conv_w, conv_b.reshape(1, C)]
    else:
        (w_in,) = w_in_list
        in_specs += [pl.BlockSpec((None, D, tf), lambda i, j: (layer, 0, j)),
                     pl.BlockSpec((None, D, tf), lambda i, j: (layer, 0, nj + j)),
                     pl.BlockSpec((None, D, tf), lambda i, j: (layer, 0, 2 * nj + j)),
                     pl.BlockSpec((3, tf), lambda i, j: (0, j))]
        args += [w_in, w_in, w_in, conv_w]
    in_specs += [pl.BlockSpec((None, tf, D), lambda i, j: (layer, j, 0)),
                 pl.BlockSpec((1, D), lambda i, j: (0, 0))]
    args += [w_out, gpost.reshape(1, D)]

    return pl.pallas_call(
        functools.partial(_gated_block_kernel, mode=mode, tm=tm, tiles_per_seq=tiles_per_seq),
        grid=(ni, nj),
        in_specs=in_specs,
        out_specs=pl.BlockSpec((tm, D), row, pipeline_mode=pl.Buffered(out_buffers)),
        out_shape=jax.ShapeDtypeStruct((M, D), F32),
        scratch_shapes=[pltpu.VMEM((tm + 2 * HALO_ROWS, D), BF16)],
        compiler_params=_params("parallel", "arbitrary"),
        name="gated_block_" + mode,
    )(*args)


def _norm_in_proj_kernel(x_ref, g_ref, w_ref, qkv_ref, f_ref, h_scr, *, n_qkv_tiles):
    j = pl.program_id(1)

    @pl.when(j == 0)
    def _():
        h_scr[...] = _rms(x_ref[...], g_ref[...]).astype(BF16)

    y = jnp.dot(h_scr[...], w_ref[...], preferred_element_type=F32)

    @pl.when(j < n_qkv_tiles)
    def _():
        for c in range(qkv_ref.shape[0]):
            qkv_ref[c] = y[:, c * LANES:(c + 1) * LANES].astype(qkv_ref.dtype)

    @pl.when(j >= n_qkv_tiles)
    def _():
        f_ref[...] = y.astype(f_ref.dtype)


def _norm_in_proj(x, g, w, layer, qkv_width, *, tm, tn):
    M, D = x.shape
    N = w.shape[2]
    assert M % tm == 0 and qkv_width % tn == 0 and (N - qkv_width) % tn == 0 and tn % LANES == 0
    n_qkv_tiles = qkv_width // tn
    cols = tn // LANES
    return pl.pallas_call(
        functools.partial(_norm_in_proj_kernel, n_qkv_tiles=n_qkv_tiles),
        grid=(M // tm, N // tn),
        in_specs=[pl.BlockSpec((tm, D), lambda i, j: (i, 0)),
                  pl.BlockSpec((1, D), lambda i, j: (0, 0)),
                  pl.BlockSpec((None, D, tn), lambda i, j: (layer, 0, j))],
        out_specs=[pl.BlockSpec((cols, tm, LANES), lambda i, j: (jnp.minimum(j, n_qkv_tiles - 1), i, 0)),
                   pl.BlockSpec((tm, tn), lambda i, j: (i, jnp.maximum(j - n_qkv_tiles, 0)))],
        out_shape=[jax.ShapeDtypeStruct((qkv_width // LANES, M, LANES), BF16),
                   jax.ShapeDtypeStruct((M, N - qkv_width), BF16)],
        scratch_shapes=[pltpu.VMEM((tm, D), BF16)],
        compiler_params=_params("parallel", "arbitrary"),
        name="norm_in_proj",
    )(x, g.reshape(1, D), w)


def _bucket_saturation_distance(num_buckets):
    nb = num_buckets // 2
    max_exact = nb // 2
    n = max_exact
    while max_exact + int(math.log(n / max_exact) / math.log(MAX_DISTANCE / max_exact)
                          * (nb - max_exact)) < nb - 1:
        n += 1
    return n


def _attn_consts_kernel(rb_ref, lq1_ref, lk1_ref, lq2_ref, lk2_ref, bias_ref, lam_ref, *, T, num_buckets):
    h = pl.program_id(0)
    nb = num_buckets // 2
    max_exact = nb // 2
    row = lax.broadcasted_iota(jnp.int32, (T, T), 0)
    col = lax.broadcasted_iota(jnp.int32, (T, T), 1)
    for t, off in enumerate((-2, -1, 0, 1, 2)):
        rel = row - col + off * T
        ret = jnp.where(rel > 0, nb, 0)
        n = jnp.abs(rel)
        nf = jnp.maximum(n, 1).astype(F32)
        large = max_exact + (jnp.log(nf / max_exact) / math.log(MAX_DISTANCE / max_exact)
                             * (nb - max_exact)).astype(jnp.int32)
        large = jnp.minimum(large, nb - 1)
        bucket = ret + jnp.where(n < max_exact, n, large)
        bias = jnp.zeros((T, T), F32)
        for b in range(num_buckets):
            bias = jnp.where(bucket == b, rb_ref[b, h], bias)
        bias_ref[0, t] = bias * LOG2_E
    s1 = jnp.sum(lq1_ref[...] * lk1_ref[...], axis=-1, keepdims=True)
    s2 = jnp.sum(lq2_ref[...] * lk2_ref[...], axis=-1, keepdims=True)
    lam_ref[...] = jnp.broadcast_to(jnp.exp(s1) - jnp.exp(s2), lam_ref.shape)


def _attn_consts(rel_bias, lq1, lk1, lq2, lk2, *, T):
    num_buckets, H = rel_bias.shape
    n_ab = lq1.shape[0]
    assert T + 1 >= _bucket_saturation_distance(num_buckets)
    vec = pl.BlockSpec(lq1.shape, lambda h: (0, 0))
    return pl.pallas_call(
        functools.partial(_attn_consts_kernel, T=T, num_buckets=num_buckets),
        grid=(H,),
        in_specs=[pl.BlockSpec(memory_space=pltpu.SMEM), vec, vec, vec, vec],
        out_specs=[pl.BlockSpec((1, 5, T, T), lambda h: (h, 0, 0, 0)),
                   pl.BlockSpec((n_ab, 128), lambda h: (0, 0))],
        out_shape=[jax.ShapeDtypeStruct((H, 5, T, T), F32),
                   jax.ShapeDtypeStruct((n_ab, 128), F32)],
        compiler_params=_params("arbitrary"),
        name="attn_consts",
    )(rel_bias, lq1, lk1, lq2, lk2)


def _attn_kernel(lam_ref, q_ref, k_ref, v_ref, bias_ref, sub_ref, o_ref,
                 qm_scr, vt_scr, *, T, bt, key_block, S, head_dim, out_scale):
    qi = pl.program_id(2)
    vd = v_ref.shape[1]

    @pl.when(qi == 0)
    def _():
        vt_scr[0:vd, :] = v_ref[...].T
        vt_scr[vd:, :] = jnp.ones((vt_scr.shape[0] - vd, vt_scr.shape[1]), BF16)

    q = q_ref[...].astype(F32) * (head_dim ** -0.5 * LOG2_E)
    lane = lax.broadcasted_iota(jnp.int32, q.shape, 1)
    qm_scr[0:T, :] = jnp.where(lane < head_dim, q, 0.0).astype(BF16)
    qm_scr[T:2 * T, :] = jnp.where(lane >= head_dim, q, 0.0).astype(BF16)
    qm = qm_scr[...]
    n_blocks = S // key_block
    key_tiles, query_tiles = key_block // bt, T // bt

    def keys_of(j):
        return slice(j * key_block, (j + 1) * key_block)

    def scores(j):
        bias = jnp.concatenate(
            [jnp.concatenate([bias_ref[0, jnp.clip(j * key_tiles + kc - (qi * query_tiles + qc), -2, 2) + 2]
                              for kc in range(key_tiles)], axis=0)
             for qc in range(query_tiles)], axis=1)
        return lax.dot_general(k_ref[keys_of(j), :], qm, (((1,), (1,)), ((), ())),
                               preferred_element_type=F32) + jnp.concatenate([bias, bias], axis=1)

    def weighted_values(j, e):
        return jnp.dot(vt_scr[:, keys_of(j)], e, preferred_element_type=F32)

    m = jnp.full((1, 2 * T), NEG_BIG, F32)
    acc = jnp.zeros((vt_scr.shape[0], 2 * T), F32)
    t_next = scores(0)
    e_prev = alpha_prev = None
    for j in range(n_blocks):
        t = t_next
        if j + 1 < n_blocks:
            t_next = scores(j + 1)
        m_new = jnp.maximum(m, jnp.max(t, axis=0, keepdims=True))
        alpha = jnp.exp2(m - m_new)
        e = jnp.exp2(t - m_new).astype(BF16)
        if j > 0:
            acc = alpha_prev * acc + weighted_values(j - 1, e_prev)
        e_prev, alpha_prev, m = e, alpha, m_new
    acc = alpha_prev * acc + weighted_values(n_blocks - 1, e_prev)
    ot = acc[0:vd] / acc[vd:vd + 1]
    lam = lam_ref[0, 0]
    o = (ot[:, 0:T] - lam * ot[:, T:2 * T]).T
    o_ref[...] = (_rms(o, sub_ref[...]) * out_scale).astype(o_ref.dtype)


def _diff_attention(qkv, lam, bias_tiles, subln, *, B, S, H, head_dim, out_scale, T, key_block):
    M = qkv.shape[1]
    vd = 2 * head_dim
    bt = bias_tiles.shape[-1]
    assert vd == LANES and S % T == 0 and S % key_block == 0 and qkv.shape[0] == 3 * H
    assert T % bt == 0 and key_block % bt == 0
    nq = S // T
    return pl.pallas_call(
        functools.partial(_attn_kernel, T=T, bt=bt, key_block=key_block, S=S, head_dim=head_dim,
                          out_scale=out_scale),
        grid=(B, H, nq),
        in_specs=[pl.BlockSpec(memory_space=pltpu.SMEM),
                  pl.BlockSpec((None, T, vd), lambda b, h, i: (h, b * nq + i, 0)),
                  pl.BlockSpec((None, S, vd), lambda b, h, i: (H + h, b, 0)),
                  pl.BlockSpec((None, S, vd), lambda b, h, i: (2 * H + h, b, 0)),
                  pl.BlockSpec((1, 5, bt, bt), lambda b, h, i: (h, 0, 0, 0)),
                  pl.BlockSpec((1, vd), lambda b, h, i: (0, 0))],
        out_specs=pl.BlockSpec((T, vd), lambda b, h, i: (b * nq + i, h)),
        out_shape=jax.ShapeDtypeStruct((M, H * vd), BF16),
        scratch_shapes=[pltpu.VMEM((2 * T, vd), BF16),
                        pltpu.VMEM((vd + BF16_SUBLANES, S), BF16)],
        compiler_params=_params("parallel", "parallel", "arbitrary"),
        name="diff_attention",
    )(lam, qkv, qkv, qkv, bias_tiles, subln.reshape(1, vd))


def _dft_tables(n, scale):
    j = jnp.arange(n, dtype=jnp.int32)[:, None]

    def trig(k):
        ang = ((j * k[None, :]) % n).astype(F32) * (2.0 * math.pi / n)
        return jnp.cos(ang), jnp.sin(ang)

    w = min(LANES, n)
    c0, s0 = trig(jnp.arange(w, dtype=jnp.int32))
    c1, s1 = trig(jnp.arange(n // w, dtype=jnp.int32) * w)
    c0, s0, c1, s1 = c0[:, None, :], s0[:, None, :], c1[:, :, None], s1[:, :, None]
    cos = (c1 * c0 - s1 * s0) * scale
    sin = (s1 * c0 + c1 * s0) * scale
    return cos.reshape(n, n).astype(BF16), sin.reshape(n, n).astype(BF16)


def _dft_channel_kernel(f_ref, cs_ref, a_ref, b_ref):
    c = f_ref.shape[1]
    y = jnp.dot(f_ref[...], cs_ref[...], preferred_element_type=F32)
    a_ref[...] = y[:, :c].astype(a_ref.dtype)
    b_ref[...] = y[:, c:].astype(b_ref.dtype)


def _dft_channel(f, cs, *, tm):
    M = f.shape[0]
    c = cs.shape[0]
    assert f.shape[1] == N_FGROUPS * c
    out = jax.ShapeDtypeStruct((M, N_FGROUPS * c), BF16)
    return pl.pallas_call(
        _dft_channel_kernel,
        grid=(M // tm, N_FGROUPS),
        in_specs=[pl.BlockSpec((tm, c), lambda i, g: (i, g)),
                  pl.BlockSpec((c, 2 * c), lambda i, g: (0, 0))],
        out_specs=[pl.BlockSpec((tm, c), lambda i, g: (i, g)),
                   pl.BlockSpec((tm, c), lambda i, g: (i, g))],
        out_shape=[out, out],
        compiler_params=_params("parallel", "parallel"),
        name="dft_channel",
    )(f, cs)


def _dft_seq_kernel(cs_ref, ss_ref, a_ref, b_ref, o_ref):
    y = (jnp.dot(cs_ref[...], a_ref[...], preferred_element_type=F32)
         - jnp.dot(ss_ref[...], b_ref[...], preferred_element_type=F32))
    o_ref[...] = y.astype(o_ref.dtype)


def _dft_seq(a, b, cos_s, sin_s, *, B, S, tm, tn):
    M, N = a.shape
    ni = S // tm
    return pl.pallas_call(
        _dft_seq_kernel,
        grid=(B, N // tn, ni),
        in_specs=[pl.BlockSpec((tm, S), lambda bb, n, i: (i, 0)),
                  pl.BlockSpec((tm, S), lambda bb, n, i: (i, 0)),
                  pl.BlockSpec((S, tn), lambda bb, n, i: (bb, n)),
                  pl.BlockSpec((S, tn), lambda bb, n, i: (bb, n))],
        out_specs=pl.BlockSpec((tm, tn), lambda bb, n, i: (bb * ni + i, n)),
        out_shape=jax.ShapeDtypeStruct((M, N), BF16),
        compiler_params=_params("parallel", "parallel", "arbitrary"),
        name="dft_seq",
    )(cos_s, sin_s, a, b)


def _out_proj_kernel(o_ref, f_ref, wa_ref, wf_ref, g_ref, x_ref, y_ref):
    m = (jnp.dot(o_ref[...], wa_ref[...], preferred_element_type=F32)
         + jnp.dot(f_ref[...], wf_ref[...], preferred_element_type=F32))
    y_ref[...] = x_ref[...] + _rms(m, g_ref[...])


def _out_proj(o, fo, w_out, layer, g, x, *, tm):
    M, D = x.shape
    ka, kf = o.shape[1], fo.shape[1]
    assert ka % kf == 0 and w_out.shape[1] == ka + kf
    return pl.pallas_call(
        _out_proj_kernel,
        grid=(M // tm,),
        in_specs=[pl.BlockSpec((tm, ka), lambda i: (i, 0)),
                  pl.BlockSpec((tm, kf), lambda i: (i, 0)),
                  pl.BlockSpec((None, ka, D), lambda i: (layer, 0, 0)),
                  pl.BlockSpec((None, kf, D), lambda i: (layer, ka // kf, 0)),
                  pl.BlockSpec((1, D), lambda i: (0, 0)),
                  pl.BlockSpec((tm, D), lambda i: (i, 0))],
        out_specs=pl.BlockSpec((tm, D), lambda i: (i, 0)),
        out_shape=jax.ShapeDtypeStruct((M, D), F32),
        compiler_params=_params("parallel"),
        name="out_proj",
    )(o, fo, w_out, w_out, g.reshape(1, D), x)


def _tiles(seq_len, layer):
    big = layer >= 2
    return dict(tm_block=min(1024 if big else 512, seq_len), tf_block=512, out_buffers=1 if big else 2,
                tm_conv=min(512, seq_len),
                tm_proj=min(1024, seq_len), tn_proj=1024,
                tm_out=min(512, seq_len), tm_dft=min(512, seq_len), tn_dft=512,
                attn_q=min(512 if big else 256, seq_len), attn_k=min(ATTN_KEY_BLOCK, seq_len))


def _run_trunk(x3, p, consts):
    B, S, D = x3.shape
    x = x3.reshape(B * S, D)
    depth = p["norm_pre_mix"].shape[0]
    H = p["rel_bias"].shape[1]
    head_dim = p["ab_lambda_q1"].shape[1]
    attn_width = H * 2 * head_dim
    fgroup = (D - attn_width) // N_FGROUPS
    cos_c, sin_c = consts["dft_channel"]
    cs_c = jnp.concatenate([cos_c, sin_c], axis=1)
    cos_s, sin_s = consts["dft_seq"][S]
    for i in range(depth):
        j = i // 2
        t = _tiles(S, i)
        if i % 2 == 0:
            lam_init = _lambda_init(i)
            qkv, f = _norm_in_proj(x, p["norm_pre_mix"][i], p["ab_w_in"], j, 3 * attn_width,
                                   tm=t["tm_proj"], tn=t["tn_proj"])
            lam = (consts["lam"][j, 0] + lam_init).reshape(1, 1)
            o = _diff_attention(qkv, lam, consts["bias_tiles"], p["ab_subln"][j], B=B, S=S, H=H,
                                head_dim=head_dim, out_scale=1.0 - lam_init, T=t["attn_q"],
                                key_block=t["attn_k"])
            fa, fb = _dft_channel(f, cs_c, tm=t["tm_proj"])
            fo = _dft_seq(fa, fb, cos_s, sin_s, B=B, S=S, tm=t["tm_dft"], tn=t["tn_dft"])
            x = _out_proj(o, fo, p["ab_w_out"], j, p["norm_post_mix"][i], x, tm=t["tm_out"])
        else:
            x = _gated_block(x, S, p["norm_pre_mix"][i], p["norm_post_mix"][i], [p["c_w_in"]],
                             p["c_conv"][j], None, p["c_w_out"], j, mode="conv",
                             tm=t["tm_conv"], tf=t["tf_block"], out_buffers=2)
        x = _gated_block(x, S, p["norm_pre_ffn"][i], p["norm_post_ffn"][i],
                         [p["ffn_w_gate"], p["ffn_w_up"]], p["ffn_conv"][i], p["ffn_conv_b"][i],
                         p["ffn_w_down"], i, mode="ffn", tm=t["tm_block"], tf=t["tf_block"],
                         out_buffers=t["out_buffers"])
    return x.reshape(B, S, D)


def kernel(x_prompt, x_sample, rel_bias, norm_pre_mix, norm_post_mix, norm_pre_ffn, norm_post_ffn, ab_w_in, ab_w_out, ab_lambda_q1, ab_lambda_k1, ab_lambda_q2, ab_lambda_k2, ab_subln, c_w_in, c_conv, c_w_out, ffn_w_gate, ffn_w_up, ffn_conv, ffn_conv_b, ffn_w_down):
    p = dict(rel_bias=rel_bias, norm_pre_mix=norm_pre_mix, norm_post_mix=norm_post_mix,
             norm_pre_ffn=norm_pre_ffn, norm_post_ffn=norm_post_ffn,
             ab_w_in=ab_w_in.astype(BF16), ab_w_out=ab_w_out.astype(BF16),
             ab_lambda_q1=ab_lambda_q1, ab_subln=ab_subln,
             c_w_in=c_w_in.astype(BF16), c_conv=c_conv, c_w_out=c_w_out.astype(BF16),
             ffn_w_gate=ffn_w_gate.astype(BF16), ffn_w_up=ffn_w_up.astype(BF16),
             ffn_conv=ffn_conv, ffn_conv_b=ffn_conv_b, ffn_w_down=ffn_w_down.astype(BF16))
    H = rel_bias.shape[1]
    attn_width = H * 2 * ab_lambda_q1.shape[1]
    fgroup = (x_prompt.shape[-1] - attn_width) // N_FGROUPS
    bias_tiles, lam = _attn_consts(rel_bias, ab_lambda_q1, ab_lambda_k1, ab_lambda_q2, ab_lambda_k2,
                                   T=ATTN_TILE)
    consts = dict(bias_tiles=bias_tiles, lam=lam,
                  dft_channel=_dft_tables(fgroup, fgroup ** -0.5),
                  dft_seq={s: _dft_tables(s, s ** -0.5)
                           for s in sorted({x_prompt.shape[1], x_sample.shape[1]})})
    return (_run_trunk(x_prompt, p, consts), _run_trunk(x_sample, p, consts))
```

```python
import functools
import math

import jax
import jax.numpy as jnp
from jax import lax
from jax.experimental import pallas as pl
from jax.experimental.pallas import tpu as pltpu

F32 = jnp.float32
BF16 = jnp.bfloat16

EPS = 1e-6
MAX_DISTANCE = 128
N_FGROUPS = 4
V7X_VMEM_LIMIT_BYTES = 60000 * 1024
HALO_ROWS = 8
ATTN_TILE = 256
ATTN_QUERIES = 512
ATTN_KEY_BLOCK = 512
NEG_BIG = -1e30
LANES = 128
NORM_ROWS = 16
NORM_UNROLL = 8
BF16_SUBLANES = 16
LOG2_E = math.log2(math.e)


def _params(*semantics):
    return pltpu.CompilerParams(dimension_semantics=semantics,
                                vmem_limit_bytes=V7X_VMEM_LIMIT_BYTES)


def _rms(x, g):
    ms = jnp.mean(x * x, axis=-1, keepdims=True)
    return (x * lax.rsqrt(ms + EPS)) * g


def _norm_rows(src_ref, g_ref, n_rows, store):
    def body(i, carry):
        rows = pl.ds(pl.multiple_of(i * NORM_ROWS, NORM_ROWS), NORM_ROWS)
        store(rows, _rms(src_ref[rows, :], g_ref[...]))
        return carry

    lax.fori_loop(0, n_rows // NORM_ROWS, body, 0, unroll=NORM_UNROLL)


def _lambda_init(layer_idx):
    return 0.8 - 0.6 * math.exp(-0.3 * layer_idx)


def _conv_rows(t, cw, tm):
    n = t.shape[0]
    prev = pltpu.roll(t, 1, 0)[:tm]
    nxt = pltpu.roll(t, n - 1, 0)[:tm]
    return prev * cw[0:1] + t[:tm] * cw[1:2] + nxt * cw[2:3]


def _gated_block_kernel(xp_ref, x_ref, xn_ref, gpre_ref, *rest, mode, tm, tiles_per_seq):
    if mode == "ffn":
        wg_ref, wu_ref, cw_ref, cb_ref, wd_ref, gpost_ref, o_ref, h_scr, acc_scr = rest
    else:
        wb_ref, wc_ref, wx_ref, cw_ref, wd_ref, gpost_ref, o_ref, h_scr, acc_scr = rest
    i = pl.program_id(0)
    j = pl.program_id(1)
    nj = pl.num_programs(1)

    @pl.when(j == 0)
    def _():
        g = gpre_ref[...]

        def store_h(rows, y):
            h_scr[rows, :] = y.astype(BF16)

        _norm_rows(x_ref, gpre_ref, tm, store_h)
        it = i % tiles_per_seq
        keep_prev = jnp.where(it != 0, 1.0, 0.0)
        keep_next = jnp.where(it != tiles_per_seq - 1, 1.0, 0.0)
        halo = jnp.concatenate([_rms(xn_ref[...], g) * keep_next,
                                _rms(xp_ref[...], g) * keep_prev], axis=0)
        h_scr[tm:tm + 2 * HALO_ROWS, :] = halo.astype(BF16)
        acc_scr[...] = jnp.zeros_like(acc_scr)

    h_all = h_scr[...]
    h_main = h_scr[0:tm, :]
    cw = cw_ref[...]
    if mode == "ffn":
        gate = jnp.dot(h_all, wg_ref[...], preferred_element_type=F32)
        up = jnp.dot(h_main, wu_ref[...], preferred_element_type=F32)
        g = _conv_rows(gate, cw, tm) + cb_ref[...]
        a = (g * jax.nn.sigmoid(g)) * up
    else:
        cg = jnp.dot(h_all, wc_ref[...], preferred_element_type=F32)
        xv = jnp.dot(h_all, wx_ref[...], preferred_element_type=F32)
        bg = jnp.dot(h_main, wb_ref[...], preferred_element_type=F32)
        a = _conv_rows(cg * xv, cw, tm) * bg
    acc_scr[...] += jnp.dot(a.astype(BF16), wd_ref[...], preferred_element_type=F32)

    @pl.when(j == nj - 1)
    def _():
        def store_o(rows, y):
            o_ref[rows, :] = x_ref[rows, :] + y

        _norm_rows(acc_scr, gpost_ref, tm, store_o)


def _gated_block(x, seq_len, gpre, gpost, w_in_list, conv_w, conv_b, w_out, layer, *, mode, tm, tf):
    M, D = x.shape
    C = w_out.shape[1]
    assert M % tm == 0 and seq_len % tm == 0 and C % tf == 0 and tm % (2 * HALO_ROWS) == 0
    ni, nj = M // tm, C // tf
    tiles_per_seq = seq_len // tm
    blocks_per_tile = tm // HALO_ROWS
    last_block = M // HALO_ROWS - 1

    row = lambda i, j: (i, 0)
    in_specs = [
        pl.BlockSpec((HALO_ROWS, D), lambda i, j: (jnp.maximum(i * blocks_per_tile - 1, 0), 0)),
        pl.BlockSpec((tm, D), row),
        pl.BlockSpec((HALO_ROWS, D), lambda i, j: (jnp.minimum((i + 1) * blocks_per_tile, last_block), 0)),
        pl.BlockSpec((1, D), lambda i, j: (0, 0)),
    ]
    args = [x, x, x, gpre.reshape(1, D)]
    if mode == "ffn":
        wg, wu = w_in_list
        in_specs += [pl.BlockSpec((None, D, tf), lambda i, j: (layer, 0, j)),
                     pl.BlockSpec((None, D, tf), lambda i, j: (layer, 0, j)),
                     pl.BlockSpec((3, tf), lambda i, j: (0, j)),
                     pl.BlockSpec((1, tf), lambda i, j: (0, j))]
        args += [wg, wu, conv_w, conv_b.reshape(1, C)]
    else:
        (w_in,) = w_in_list
        in_specs += [pl.BlockSpec((None, D, tf), lambda i, j: (layer, 0, j)),
                     pl.BlockSpec((None, D, tf), lambda i, j: (layer, 0, nj + j)),
                     pl.BlockSpec((None, D, tf), lambda i, j: (layer, 0, 2 * nj + j)),
                     pl.BlockSpec((3, tf), lambda i, j: (0, j))]
        args += [w_in, w_in, w_in, conv_w]
    in_specs += [pl.BlockSpec((None, tf, D), lambda i, j: (layer, j, 0)),
                 pl.BlockSpec((1, D), lambda i, j: (0, 0))]
    args += [w_out, gpost.reshape(1, D)]

    return pl.pallas_call(
        functools.partial(_gated_block_kernel, mode=mode, tm=tm, tiles_per_seq=tiles_per_seq),
        grid=(ni, nj),
        in_specs=in_specs,
        out_specs=pl.BlockSpec((tm, D), row),
        out_shape=jax.ShapeDtypeStruct((M, D), F32),
        scratch_shapes=[pltpu.VMEM((tm + 2 * HALO_ROWS, D), BF16),
                        pltpu.VMEM((tm, D), F32)],
        compiler_params=_params("parallel", "arbitrary"),
        name="gated_block_" + mode,
    )(*args)


def _norm_in_proj_kernel(x_ref, g_ref, w_ref, cs_ref, qkv_ref, fa_ref, fb_ref, h_scr, *, n_qkv_tiles):
    j = pl.program_id(1)

    @pl.when(j == 0)
    def _():
        def store_h(rows, y):
            h_scr[rows, :] = y.astype(BF16)

        _norm_rows(x_ref, g_ref, h_scr.shape[0], store_h)

    y = jnp.dot(h_scr[...], w_ref[...], preferred_element_type=F32)

    @pl.when(j < n_qkv_tiles)
    def _():
        for c in range(qkv_ref.shape[0]):
            qkv_ref[c] = y[:, c * LANES:(c + 1) * LANES].astype(qkv_ref.dtype)

    @pl.when(j >= n_qkv_tiles)
    def _():
        c = cs_ref.shape[0]
        f = y.astype(BF16)
        for grp in range(N_FGROUPS):
            cols = slice(grp * c, (grp + 1) * c)
            ab = jnp.dot(f[:, cols], cs_ref[...], preferred_element_type=F32)
            fa_ref[:, cols] = ab[:, :c].astype(fa_ref.dtype)
            fb_ref[:, cols] = ab[:, c:].astype(fb_ref.dtype)


def _norm_in_proj(x, g, w, layer, qkv_width, cs, *, tm, tn):
    M, D = x.shape
    N = w.shape[2]
    fw = N - qkv_width
    assert M % tm == 0 and qkv_width % tn == 0 and fw == tn and tn % LANES == 0
    assert cs.shape == (fw // N_FGROUPS, 2 * fw // N_FGROUPS)
    n_qkv_tiles = qkv_width // tn
    cols = tn // LANES
    f_out = jax.ShapeDtypeStruct((M, fw), BF16)
    f_spec = pl.BlockSpec((tm, fw), lambda i, j: (i, 0))
    return pl.pallas_call(
        functools.partial(_norm_in_proj_kernel, n_qkv_tiles=n_qkv_tiles),
        grid=(M // tm, N // tn),
        in_specs=[pl.BlockSpec((tm, D), lambda i, j: (i, 0)),
                  pl.BlockSpec((1, D), lambda i, j: (0, 0)),
                  pl.BlockSpec((None, D, tn), lambda i, j: (layer, 0, j)),
                  pl.BlockSpec(cs.shape, lambda i, j: (0, 0))],
        out_specs=[pl.BlockSpec((cols, tm, LANES), lambda i, j: (jnp.minimum(j, n_qkv_tiles - 1), i, 0)),
                   f_spec, f_spec],
        out_shape=[jax.ShapeDtypeStruct((qkv_width // LANES, M, LANES), BF16), f_out, f_out],
        scratch_shapes=[pltpu.VMEM((tm, D), BF16)],
        compiler_params=_params("parallel", "arbitrary"),
        name="norm_in_proj",
    )(x, g.reshape(1, D), w, cs)


def _bucket_saturation_distance(num_buckets):
    nb = num_buckets // 2
    max_exact = nb // 2
    n = max_exact
    while max_exact + int(math.log(n / max_exact) / math.log(MAX_DISTANCE / max_exact)
                          * (nb - max_exact)) < nb - 1:
        n += 1
    return n


def _attn_consts_kernel(rb_ref, lq1_ref, lk1_ref, lq2_ref, lk2_ref, bias_ref, lam_ref, *, T, num_buckets):
    h = pl.program_id(0)
    nb = num_buckets // 2
    max_exact = nb // 2
    row = lax.broadcasted_iota(jnp.int32, (T, T), 0)
    col = lax.broadcasted_iota(jnp.int32, (T, T), 1)
    for t, off in enumerate((-2, -1, 0, 1, 2)):
        rel = row - col + off * T
        ret = jnp.where(rel > 0, nb, 0)
        n = jnp.abs(rel)
        nf = jnp.maximum(n, 1).astype(F32)
        large = max_exact + (jnp.log(nf / max_exact) / math.log(MAX_DISTANCE / max_exact)
                             * (nb - max_exact)).astype(jnp.int32)
        large = jnp.minimum(large, nb - 1)
        bucket = ret + jnp.where(n < max_exact, n, large)
        bias = jnp.zeros((T, T), F32)
        for b in range(num_buckets):
            bias = jnp.where(bucket == b, rb_ref[b, h], bias)
        bias_ref[0, t] = bias * LOG2_E
    s1 = jnp.sum(lq1_ref[...] * lk1_ref[...], axis=-1, keepdims=True)
    s2 = jnp.sum(lq2_ref[...] * lk2_ref[...], axis=-1, keepdims=True)
    lam_ref[...] = jnp.broadcast_to(jnp.exp(s1) - jnp.exp(s2), lam_ref.shape)


def _attn_consts(rel_bias, lq1, lk1, lq2, lk2, *, T):
    num_buckets, H = rel_bias.shape
    n_ab = lq1.shape[0]
    assert T + 1 >= _bucket_saturation_distance(num_buckets)
    vec = pl.BlockSpec(lq1.shape, lambda h: (0, 0))
    return pl.pallas_call(
        functools.partial(_attn_consts_kernel, T=T, num_buckets=num_buckets),
        grid=(H,),
        in_specs=[pl.BlockSpec(memory_space=pltpu.SMEM), vec, vec, vec, vec],
        out_specs=[pl.BlockSpec((1, 5, T, T), lambda h: (h, 0, 0, 0)),
                   pl.BlockSpec((n_ab, 128), lambda h: (0, 0))],
        out_shape=[jax.ShapeDtypeStruct((H, 5, T, T), F32),
                   jax.ShapeDtypeStruct((n_ab, 128), F32)],
        compiler_params=_params("arbitrary"),
        name="attn_consts",
    )(rel_bias, lq1, lk1, lq2, lk2)


def _attn_kernel(lam_ref, q_ref, k_ref, v_ref, bias_ref, sub_ref, o_ref,
                 qm_scr, vt_scr, *, T, bt, key_block, S, head_dim, out_scale):
    qi = pl.program_id(2)
    vd = v_ref.shape[1]

    @pl.when(qi == 0)
    def _():
        vt_scr[0:vd, :] = v_ref[...].T
        vt_scr[vd:, :] = jnp.ones((vt_scr.shape[0] - vd, vt_scr.shape[1]), BF16)

    q = q_ref[...].astype(F32) * (head_dim ** -0.5 * LOG2_E)
    lane = lax.broadcasted_iota(jnp.int32, q.shape, 1)
    qm_scr[0:T, :] = jnp.where(lane < head_dim, q, 0.0).astype(BF16)
    qm_scr[T:2 * T, :] = jnp.where(lane >= head_dim, q, 0.0).astype(BF16)
    qm = qm_scr[...]
    n_blocks = S // key_block
    key_tiles, query_tiles = key_block // bt, T // bt

    def keys_of(j):
        return slice(j * key_block, (j + 1) * key_block)

    def scores(j):
        bias = jnp.concatenate(
            [jnp.concatenate([bias_ref[0, jnp.clip(j * key_tiles + kc - (qi * query_tiles + qc), -2, 2) + 2]
                              for kc in range(key_tiles)], axis=0)
             for qc in range(query_tiles)], axis=1)
        t = lax.dot_general(k_ref[keys_of(j), :], qm, (((1,), (1,)), ((), ())),
                            preferred_element_type=F32) + jnp.concatenate([bias, bias], axis=1)
        return t, jnp.max(t, axis=0, keepdims=True)

    def weighted_values(j, e):
        return jnp.dot(vt_scr[:, keys_of(j)], e, preferred_element_type=F32)

    m = jnp.full((1, 2 * T), NEG_BIG, F32)
    acc = jnp.zeros((vt_scr.shape[0], 2 * T), F32)
    t_next, mx_next = scores(0)
    e_prev = alpha_prev = None
    for j in range(n_blocks):
        t, mx = t_next, mx_next
        if j + 1 < n_blocks:
            t_next, mx_next = scores(j + 1)
        m_new = jnp.maximum(m, mx)
        alpha = jnp.exp2(m - m_new)
        e = jnp.exp2(t - m_new).astype(BF16)
        if j > 0:
            acc = alpha_prev * acc + weighted_values(j - 1, e_prev)
        e_prev, alpha_prev, m = e, alpha, m_new
    acc = alpha_prev * acc + weighted_values(n_blocks - 1, e_prev)
    ot = acc[0:vd] / acc[vd:vd + 1]
    lam = lam_ref[0, 0]
    o = (ot[:, 0:T] - lam * ot[:, T:2 * T]).T
    o_ref[...] = (_rms(o, sub_ref[...]) * out_scale).astype(o_ref.dtype)


def _diff_attention(qkv, lam, bias_tiles, subln, *, B, S, H, head_dim, out_scale, T, key_block):
    M = qkv.shape[1]
    vd = 2 * head_dim
    bt = bias_tiles.shape[-1]
    assert vd == LANES and S % T == 0 and S % key_block == 0 and qkv.shape[0] == 3 * H
    assert T % bt == 0 and key_block % bt == 0
    nq = S // T
    return pl.pallas_call(
        functools.partial(_attn_kernel, T=T, bt=bt, key_block=key_block, S=S, head_dim=head_dim,
                          out_scale=out_scale),
        grid=(B, H, nq),
        in_specs=[pl.BlockSpec(memory_space=pltpu.SMEM),
                  pl.BlockSpec((None, T, vd), lambda b, h, i: (h, b * nq + i, 0)),
                  pl.BlockSpec((None, S, vd), lambda b, h, i: (H + h, b, 0)),
                  pl.BlockSpec((None, S, vd), lambda b, h, i: (2 * H + h, b, 0)),
                  pl.BlockSpec((1, 5, bt, bt), lambda b, h, i: (h, 0, 0, 0)),
                  pl.BlockSpec((1, vd), lambda b, h, i: (0, 0))],
        out_specs=pl.BlockSpec((T, vd), lambda b, h, i: (b * nq + i, h)),
        out_shape=jax.ShapeDtypeStruct((M, H * vd), BF16),
        scratch_shapes=[pltpu.VMEM((2 * T, vd), BF16),
                        pltpu.VMEM((vd + BF16_SUBLANES, S), BF16)],
        compiler_params=_params("parallel", "parallel", "arbitrary"),
        name="diff_attention",
    )(lam, qkv, qkv, qkv, bias_tiles, subln.reshape(1, vd))


def _dft_tables(n, scale):
    j = jnp.arange(n, dtype=jnp.int32)[:, None]

    def trig(k):
        ang = ((j * k[None, :]) % n).astype(F32) * (2.0 * math.pi / n)
        return jnp.cos(ang), jnp.sin(ang)

    w = min(LANES, n)
    c0, s0 = trig(jnp.arange(w, dtype=jnp.int32))
    c1, s1 = trig(jnp.arange(n // w, dtype=jnp.int32) * w)
    c0, s0, c1, s1 = c0[:, None, :], s0[:, None, :], c1[:, :, None], s1[:, :, None]
    cos = (c1 * c0 - s1 * s0) * scale
    sin = (s1 * c0 + c1 * s0) * scale
    return cos.reshape(n, n).astype(BF16), sin.reshape(n, n).astype(BF16)


def _dft_seq_kernel(cs_ref, ss_ref, a_ref, b_ref, o_ref):
    y = (jnp.dot(cs_ref[...], a_ref[...], preferred_element_type=F32)
         - jnp.dot(ss_ref[...], b_ref[...], preferred_element_type=F32))
    o_ref[...] = y.astype(o_ref.dtype)


def _dft_seq(a, b, cos_s, sin_s, *, B, S, tm, tn):
    M, N = a.shape
    ni = S // tm
    return pl.pallas_call(
        _dft_seq_kernel,
        grid=(B, N // tn, ni),
        in_specs=[pl.BlockSpec((tm, S), lambda bb, n, i: (i, 0)),
                  pl.BlockSpec((tm, S), lambda bb, n, i: (i, 0)),
                  pl.BlockSpec((S, tn), lambda bb, n, i: (bb, n)),
                  pl.BlockSpec((S, tn), lambda bb, n, i: (bb, n))],
        out_specs=pl.BlockSpec((tm, tn), lambda bb, n, i: (bb * ni + i, n)),
        out_shape=jax.ShapeDtypeStruct((M, N), BF16),
        compiler_params=_params("parallel", "parallel", "arbitrary"),
        name="dft_seq",
    )(cos_s, sin_s, a, b)


def _out_proj_kernel(o_ref, f_ref, wa_ref, wf_ref, g_ref, x_ref, y_ref):
    m = (jnp.dot(o_ref[...], wa_ref[...], preferred_element_type=F32)
         + jnp.dot(f_ref[...], wf_ref[...], preferred_element_type=F32))
    y_ref[...] = x_ref[...] + _rms(m, g_ref[...])


def _out_proj(o, fo, w_out, layer, g, x, *, tm):
    M, D = x.shape
    ka, kf = o.shape[1], fo.shape[1]
    assert ka % kf == 0 and w_out.shape[1] == ka + kf
    return pl.pallas_call(
        _out_proj_kernel,
        grid=(M // tm,),
        in_specs=[pl.BlockSpec((tm, ka), lambda i: (i, 0)),
                  pl.BlockSpec((tm, kf), lambda i: (i, 0)),
                  pl.BlockSpec((None, ka, D), lambda i: (layer, 0, 0)),
                  pl.BlockSpec((None, kf, D), lambda i: (layer, ka // kf, 0)),
                  pl.BlockSpec((1, D), lambda i: (0, 0)),
                  pl.BlockSpec((tm, D), lambda i: (i, 0))],
        out_specs=pl.BlockSpec((tm, D), lambda i: (i, 0)),
        out_shape=jax.ShapeDtypeStruct((M, D), F32),
        compiler_params=_params("parallel"),
        name="out_proj",
    )(o, fo, w_out, w_out, g.reshape(1, D), x)


def _tiles(seq_len):
    return dict(tm_block=min(512, seq_len), tf_block=512, tm_proj=min(1024, seq_len), tn_proj=1024,
                tm_out=min(512, seq_len), tm_dft=min(512, seq_len), tn_dft=512,
                attn_q=min(ATTN_QUERIES, seq_len), attn_k=min(ATTN_KEY_BLOCK, seq_len))


def _run_trunk(x3, p, consts):
    B, S, D = x3.shape
    x = x3.reshape(B * S, D)
    t = _tiles(S)
    depth = p["norm_pre_mix"].shape[0]
    H = p["rel_bias"].shape[1]
    head_dim = p["ab_lambda_q1"].shape[1]
    attn_width = H * 2 * head_dim
    cos_c, sin_c = consts["dft_channel"]
    cs_c = jnp.concatenate([cos_c, sin_c], axis=1)
    cos_s, sin_s = consts["dft_seq"][S]
    for i in range(depth):
        j = i // 2
        if i % 2 == 0:
            lam_init = _lambda_init(i)
            qkv, fa, fb = _norm_in_proj(x, p["norm_pre_mix"][i], p["ab_w_in"], j, 3 * attn_width, cs_c,
                                        tm=t["tm_proj"], tn=t["tn_proj"])
            lam = (consts["lam"][j, 0] + lam_init).reshape(1, 1)
            o = _diff_attention(qkv, lam, consts["bias_tiles"], p["ab_subln"][j], B=B, S=S, H=H,
                                head_dim=head_dim, out_scale=1.0 - lam_init, T=t["attn_q"],
                                key_block=t["attn_k"])
            fo = _dft_seq(fa, fb, cos_s, sin_s, B=B, S=S, tm=t["tm_dft"], tn=t["tn_dft"])
            x = _out_proj(o, fo, p["ab_w_out"], j, p["norm_post_mix"][i], x, tm=t["tm_out"])
        else:
            x = _gated_block(x, S, p["norm_pre_mix"][i], p["norm_post_mix"][i], [p["c_w_in"]],
                             p["c_conv"][j], None, p["c_w_out"], j, mode="conv",
                             tm=t["tm_block"], tf=t["tf_block"])
        x = _gated_block(x, S, p["norm_pre_ffn"][i], p["norm_post_ffn"][i],
                         [p["ffn_w_gate"], p["ffn_w_up"]], p["ffn_conv"][i], p["ffn_conv_b"][i],
                         p["ffn_w_down"], i, mode="ffn", tm=t["tm_block"], tf=t["tf_block"])
    return x.reshape(B, S, D)


def kernel(x_prompt, x_sample, rel_bias, norm_pre_mix, norm_post_mix, norm_pre_ffn, norm_post_ffn, ab_w_in, ab_w_out, ab_lambda_q1, ab_lambda_k1, ab_lambda_q2, ab_lambda_k2, ab_subln, c_w_in, c_conv, c_w_out, ffn_w_gate, ffn_w_up, ffn_conv, ffn_conv_b, ffn_w_down):
    p = dict(rel_bias=rel_bias, norm_pre_mix=norm_pre_mix, norm_post_mix=norm_post_mix,
             norm_pre_ffn=norm_pre_ffn, norm_post_ffn=norm_post_ffn,
             ab_w_in=ab_w_in.astype(BF16), ab_w_out=ab_w_out.astype(BF16),
             ab_lambda_q1=ab_lambda_q1, ab_subln=ab_subln,
             c_w_in=c_w_in.astype(BF16), c_conv=c_conv, c_w_out=c_w_out.astype(BF16),
             ffn_w_gate=ffn_w_gate.astype(BF16), ffn_w_up=ffn_w_up.astype(BF16),
             ffn_conv=ffn_conv, ffn_conv_b=ffn_conv_b, ffn_w_down=ffn_w_down.astype(BF16))
    H = rel_bias.shape[1]
    attn_width = H * 2 * ab_lambda_q1.shape[1]
    fgroup = (x_prompt.shape[-1] - attn_width) // N_FGROUPS
    bias_tiles, lam = _attn_consts(rel_bias, ab_lambda_q1, ab_lambda_k1, ab_lambda_q2, ab_lambda_k2,
                                   T=ATTN_TILE)
    consts = dict(bias_tiles=bias_tiles, lam=lam,
                  dft_channel=_dft_tables(fgroup, fgroup ** -0.5),
                  dft_seq={s: _dft_tables(s, s ** -0.5)
                           for s in sorted({x_prompt.shape[1], x_sample.shape[1]})})
    return (_run_trunk(x_prompt, p, consts), _run_trunk(x_sample, p, consts))
```

```python
import functools
import math

import jax
import jax.numpy as jnp
from jax import lax
from jax.experimental import pallas as pl
from jax.experimental.pallas import tpu as pltpu

F32 = jnp.float32
BF16 = jnp.bfloat16

EPS = 1e-6
MAX_DISTANCE = 128
N_FGROUPS = 4
V7X_VMEM_LIMIT_BYTES = 60000 * 1024
HALO_ROWS = 8
ATTN_TILE = 256
ATTN_QUERIES = 512
ATTN_KEY_BLOCK = 512
NEG_BIG = -1e30
LANES = 128
NORM_ROWS = 16
NORM_UNROLL = 8
BF16_SUBLANES = 16
LOG2_E = math.log2(math.e)


def _params(*semantics):
    return pltpu.CompilerParams(dimension_semantics=semantics,
                                vmem_limit_bytes=V7X_VMEM_LIMIT_BYTES)


def _rms(x, g):
    ms = jnp.mean(x * x, axis=-1, keepdims=True)
    return (x * lax.rsqrt(ms + EPS)) * g


def _norm_rows(src_ref, g_ref, n_rows, store):
    def body(i, carry):
        rows = pl.ds(pl.multiple_of(i * NORM_ROWS, NORM_ROWS), NORM_ROWS)
        store(rows, _rms(src_ref[rows, :], g_ref[...]))
        return carry

    lax.fori_loop(0, n_rows // NORM_ROWS, body, 0, unroll=NORM_UNROLL)


def _lambda_init(layer_idx):
    return 0.8 - 0.6 * math.exp(-0.3 * layer_idx)


def _conv_rows(t, cw, tm):
    n = t.shape[0]
    prev = pltpu.roll(t, 1, 0)[:tm]
    nxt = pltpu.roll(t, n - 1, 0)[:tm]
    return prev * cw[0:1] + t[:tm] * cw[1:2] + nxt * cw[2:3]


def _gated_block_kernel(xp_ref, x_ref, xn_ref, gpre_ref, *rest, mode, tm, tiles_per_seq):
    if mode == "ffn":
        wg_ref, wu_ref, cw_ref, cb_ref, wd_ref, gpost_ref, o_ref, h_scr, acc_scr = rest
    else:
        wb_ref, wc_ref, wx_ref, cw_ref, wd_ref, gpost_ref, o_ref, h_scr, acc_scr = rest
    i = pl.program_id(0)
    j = pl.program_id(1)
    nj = pl.num_programs(1)

    @pl.when(j == 0)
    def _():
        g = gpre_ref[...]

        def store_h(rows, y):
            h_scr[rows, :] = y.astype(BF16)

        _norm_rows(x_ref, gpre_ref, tm, store_h)
        it = i % tiles_per_seq
        keep_prev = jnp.where(it != 0, 1.0, 0.0)
        keep_next = jnp.where(it != tiles_per_seq - 1, 1.0, 0.0)
        halo = jnp.concatenate([_rms(xn_ref[...], g) * keep_next,
                                _rms(xp_ref[...], g) * keep_prev], axis=0)
        h_scr[tm:tm + 2 * HALO_ROWS, :] = halo.astype(BF16)
        acc_scr[...] = jnp.zeros_like(acc_scr)

    h_all = h_scr[...]
    h_main = h_scr[0:tm, :]
    cw = cw_ref[...]
    if mode == "ffn":
        gate = jnp.dot(h_all, wg_ref[...], preferred_element_type=F32)
        up = jnp.dot(h_main, wu_ref[...], preferred_element_type=F32)
        g = _conv_rows(gate, cw, tm) + cb_ref[...]
        a = (g * jax.nn.sigmoid(g)) * up
    else:
        cg = jnp.dot(h_all, wc_ref[...], preferred_element_type=F32)
        xv = jnp.dot(h_all, wx_ref[...], preferred_element_type=F32)
        bg = jnp.dot(h_main, wb_ref[...], preferred_element_type=F32)
        a = _conv_rows(cg * xv, cw, tm) * bg
    acc_scr[...] += jnp.dot(a.astype(BF16), wd_ref[...], preferred_element_type=F32)

    @pl.when(j == nj - 1)
    def _():
        def store_o(rows, y):
            o_ref[rows, :] = x_ref[rows, :] + y

        _norm_rows(acc_scr, gpost_ref, tm, store_o)


def _gated_block(x, seq_len, gpre, gpost, w_in_list, conv_w, conv_b, w_out, layer, *, mode, tm, tf):
    M, D = x.shape
    C = w_out.shape[1]
    assert M % tm == 0 and seq_len % tm == 0 and C % tf == 0 and tm % (2 * HALO_ROWS) == 0
    ni, nj = M // tm, C // tf
    tiles_per_seq = seq_len // tm
    blocks_per_tile = tm // HALO_ROWS
    last_block = M // HALO_ROWS - 1

    row = lambda i, j: (i, 0)
    in_specs = [
        pl.BlockSpec((HALO_ROWS, D), lambda i, j: (jnp.maximum(i * blocks_per_tile - 1, 0), 0)),
        pl.BlockSpec((tm, D), row),
        pl.BlockSpec((HALO_ROWS, D), lambda i, j: (jnp.minimum((i + 1) * blocks_per_tile, last_block), 0)),
        pl.BlockSpec((1, D), lambda i, j: (0, 0)),
    ]
    args = [x, x, x, gpre.reshape(1, D)]
    if mode == "ffn":
        wg, wu = w_in_list
        in_specs += [pl.BlockSpec((None, D, tf), lambda i, j: (layer, 0, j)),
                     pl.BlockSpec((None, D, tf), lambda i, j: (layer, 0, j)),
                     pl.BlockSpec((3, tf), lambda i, j: (0, j)),
                     pl.BlockSpec((1, tf), lambda i, j: (0, j))]
        args += [wg, wu, conv_w, conv_b.reshape(1, C)]
    else:
        (w_in,) = w_in_list
        in_specs += [pl.BlockSpec((None, D, tf), lambda i, j: (layer, 0, j)),
                     pl.BlockSpec((None, D, tf), lambda i, j: (layer, 0, nj + j)),
                     pl.BlockSpec((None, D, tf), lambda i, j: (layer, 0, 2 * nj + j)),
                     pl.BlockSpec((3, tf), lambda i, j: (0, j))]
        args += [w_in, w_in, w_in, conv_w]
    in_specs += [pl.BlockSpec((None, tf, D), lambda i, j: (layer, j, 0)),
                 pl.BlockSpec((1, D), lambda i, j: (0, 0))]
    args += [w_out, gpost.reshape(1, D)]

    return pl.pallas_call(
        functools.partial(_gated_block_kernel, mode=mode, tm=tm, tiles_per_seq=tiles_per_seq),
        grid=(ni, nj),
        in_specs=in_specs,
        out_specs=pl.BlockSpec((tm, D), row),
        out_shape=jax.ShapeDtypeStruct((M, D), F32),
        scratch_shapes=[pltpu.VMEM((tm + 2 * HALO_ROWS, D), BF16),
                        pltpu.VMEM((tm, D), F32)],
        compiler_params=_params("parallel", "arbitrary"),
        name="gated_block_" + mode,
    )(*args)


def _norm_in_proj_kernel(x_ref, g_ref, w_ref, cs_ref, qkv_ref, fa_ref, fb_ref, h_scr, *, n_qkv_tiles):
    j = pl.program_id(1)

    @pl.when(j == 0)
    def _():
        def store_h(rows, y):
            h_scr[rows, :] = y.astype(BF16)

        _norm_rows(x_ref, g_ref, h_scr.shape[0], store_h)

    y = jnp.dot(h_scr[...], w_ref[...], preferred_element_type=F32)

    @pl.when(j < n_qkv_tiles)
    def _():
        for c in range(qkv_ref.shape[0]):
            qkv_ref[c] = y[:, c * LANES:(c + 1) * LANES].astype(qkv_ref.dtype)

    @pl.when(j >= n_qkv_tiles)
    def _():
        c = cs_ref.shape[0]
        f = y.astype(BF16)
        for grp in range(N_FGROUPS):
            cols = slice(grp * c, (grp + 1) * c)
            ab = jnp.dot(f[:, cols], cs_ref[...], preferred_element_type=F32)
            fa_ref[:, cols] = ab[:, :c].astype(fa_ref.dtype)
            fb_ref[:, cols] = ab[:, c:].astype(fb_ref.dtype)


def _norm_in_proj(x, g, w, layer, qkv_width, cs, *, tm, tn):
    M, D = x.shape
    N = w.shape[2]
    fw = N - qkv_width
    assert M % tm == 0 and qkv_width % tn == 0 and fw == tn and tn % LANES == 0
    assert cs.shape == (fw // N_FGROUPS, 2 * fw // N_FGROUPS)
    n_qkv_tiles = qkv_width // tn
    cols = tn // LANES
    f_out = jax.ShapeDtypeStruct((M, fw), BF16)
    f_spec = pl.BlockSpec((tm, fw), lambda i, j: (i, 0))
    return pl.pallas_call(
        functools.partial(_norm_in_proj_kernel, n_qkv_tiles=n_qkv_tiles),
        grid=(M // tm, N // tn),
        in_specs=[pl.BlockSpec((tm, D), lambda i, j: (i, 0)),
                  pl.BlockSpec((1, D), lambda i, j: (0, 0)),
                  pl.BlockSpec((None, D, tn), lambda i, j: (layer, 0, j)),
                  pl.BlockSpec(cs.shape, lambda i, j: (0, 0))],
        out_specs=[pl.BlockSpec((cols, tm, LANES), lambda i, j: (jnp.minimum(j, n_qkv_tiles - 1), i, 0)),
                   f_spec, f_spec],
        out_shape=[jax.ShapeDtypeStruct((qkv_width // LANES, M, LANES), BF16), f_out, f_out],
        scratch_shapes=[pltpu.VMEM((tm, D), BF16)],
        compiler_params=_params("parallel", "arbitrary"),
        name="norm_in_proj",
    )(x, g.reshape(1, D), w, cs)


def _bucket_saturation_distance(num_buckets):
    nb = num_buckets // 2
    max_exact = nb // 2
    n = max_exact
    while max_exact + int(math.log(n / max_exact) / math.log(MAX_DISTANCE / max_exact)
                          * (nb - max_exact)) < nb - 1:
        n += 1
    return n


def _attn_consts_kernel(rb_ref, lq1_ref, lk1_ref, lq2_ref, lk2_ref, bias_ref, lam_ref, *, T, num_buckets):
    h = pl.program_id(0)
    nb = num_buckets // 2
    max_exact = nb // 2
    row = lax.broadcasted_iota(jnp.int32, (T, T), 0)
    col = lax.broadcasted_iota(jnp.int32, (T, T), 1)
    for t, off in enumerate((-2, -1, 0, 1, 2)):
        rel = row - col + off * T
        ret = jnp.where(rel > 0, nb, 0)
        n = jnp.abs(rel)
        nf = jnp.maximum(n, 1).astype(F32)
        large = max_exact + (jnp.log(nf / max_exact) / math.log(MAX_DISTANCE / max_exact)
                             * (nb - max_exact)).astype(jnp.int32)
        large = jnp.minimum(large, nb - 1)
        bucket = ret + jnp.where(n < max_exact, n, large)
        bias = jnp.zeros((T, T), F32)
        for b in range(num_buckets):
            bias = jnp.where(bucket == b, rb_ref[b, h], bias)
        bias_ref[0, t] = bias * LOG2_E
    s1 = jnp.sum(lq1_ref[...] * lk1_ref[...], axis=-1, keepdims=True)
    s2 = jnp.sum(lq2_ref[...] * lk2_ref[...], axis=-1, keepdims=True)
    lam_ref[...] = jnp.broadcast_to(jnp.exp(s1) - jnp.exp(s2), lam_ref.shape)


def _attn_consts(rel_bias, lq1, lk1, lq2, lk2, *, T):
    num_buckets, H = rel_bias.shape
    n_ab = lq1.shape[0]
    assert T + 1 >= _bucket_saturation_distance(num_buckets)
    vec = pl.BlockSpec(lq1.shape, lambda h: (0, 0))
    return pl.pallas_call(
        functools.partial(_attn_consts_kernel, T=T, num_buckets=num_buckets),
        grid=(H,),
        in_specs=[pl.BlockSpec(memory_space=pltpu.SMEM), vec, vec, vec, vec],
        out_specs=[pl.BlockSpec((1, 5, T, T), lambda h: (h, 0, 0, 0)),
                   pl.BlockSpec((n_ab, 128), lambda h: (0, 0))],
        out_shape=[jax.ShapeDtypeStruct((H, 5, T, T), F32),
                   jax.ShapeDtypeStruct((n_ab, 128), F32)],
        compiler_params=_params("arbitrary"),
        name="attn_consts",
    )(rel_bias, lq1, lk1, lq2, lk2)


def _attn_kernel(lam_ref, q_ref, k_ref, v_ref, bias_ref, sub_ref, o_ref,
                 qm_scr, vt_scr, *, T, bt, key_block, S, head_dim, out_scale):
    qi = pl.program_id(2)
    vd = v_ref.shape[1]

    @pl.when(qi == 0)
    def _():
        vt_scr[0:vd, :] = v_ref[...].T
        vt_scr[vd:, :] = jnp.ones((vt_scr.shape[0] - vd, vt_scr.shape[1]), BF16)

    q = q_ref[...].astype(F32) * (head_dim ** -0.5 * LOG2_E)
    lane = lax.broadcasted_iota(jnp.int32, q.shape, 1)
    qm_scr[0:T, :] = jnp.where(lane < head_dim, q, 0.0).astype(BF16)
    qm_scr[T:2 * T, :] = jnp.where(lane >= head_dim, q, 0.0).astype(BF16)
    qm = qm_scr[...]
    n_blocks = S // key_block
    key_tiles, query_tiles = key_block // bt, T // bt

    def keys_of(j):
        return slice(j * key_block, (j + 1) * key_block)

    def scores(j):
        bias = jnp.concatenate(
            [jnp.concatenate([bias_ref[0, jnp.clip(j * key_tiles + kc - (qi * query_tiles + qc), -2, 2) + 2]
                              for kc in range(key_tiles)], axis=0)
             for qc in range(query_tiles)], axis=1)
        t = lax.dot_general(k_ref[keys_of(j), :], qm, (((1,), (1,)), ((), ())),
                            preferred_element_type=F32) + jnp.concatenate([bias, bias], axis=1)
        return t, jnp.max(t, axis=0, keepdims=True)

    def weighted_values(j, e):
        return jnp.dot(vt_scr[:, keys_of(j)], e, preferred_element_type=F32)

    m = jnp.full((1, 2 * T), NEG_BIG, F32)
    acc = jnp.zeros((vt_scr.shape[0], 2 * T), F32)
    t_next, mx_next = scores(0)
    e_prev = alpha_prev = None
    for j in range(n_blocks):
        t, mx = t_next, mx_next
        if j + 1 < n_blocks:
            t_next, mx_next = scores(j + 1)
        m_new = jnp.maximum(m, mx)
        alpha = jnp.exp2(m - m_new)
        e = jnp.exp2(t - m_new).astype(BF16)
        if j > 0:
            acc = alpha_prev * acc + weighted_values(j - 1, e_prev)
        e_prev, alpha_prev, m = e, alpha, m_new
    acc = alpha_prev * acc + weighted_values(n_blocks - 1, e_prev)
    ot = acc[0:vd] / acc[vd:vd + 1]
    lam = lam_ref[0, 0]
    o = (ot[:, 0:T] - lam * ot[:, T:2 * T]).T
    o_ref[...] = (_rms(o, sub_ref[...]) * out_scale).astype(o_ref.dtype)


def _diff_attention(qkv, lam, bias_tiles, subln, *, B, S, H, head_dim, out_scale, T, key_block):
    M = qkv.shape[1]
    vd = 2 * head_dim
    bt = bias_tiles.shape[-1]
    assert vd == LANES and S % T == 0 and S % key_block == 0 and qkv.shape[0] == 3 * H
    assert T % bt == 0 and key_block % bt == 0
    nq = S // T
    return pl.pallas_call(
        functools.partial(_attn_kernel, T=T, bt=bt, key_block=key_block, S=S, head_dim=head_dim,
                          out_scale=out_scale),
        grid=(B, H, nq),
        in_specs=[pl.BlockSpec(memory_space=pltpu.SMEM),
                  pl.BlockSpec((None, T, vd), lambda b, h, i: (h, b * nq + i, 0)),
                  pl.BlockSpec((None, S, vd), lambda b, h, i: (H + h, b, 0)),
                  pl.BlockSpec((None, S, vd), lambda b, h, i: (2 * H + h, b, 0)),
                  pl.BlockSpec((1, 5, bt, bt), lambda b, h, i: (h, 0, 0, 0)),
                  pl.BlockSpec((1, vd), lambda b, h, i: (0, 0))],
        out_specs=pl.BlockSpec((T, vd), lambda b, h, i: (b * nq + i, h)),
        out_shape=jax.ShapeDtypeStruct((M, H * vd), BF16),
        scratch_shapes=[pltpu.VMEM((2 * T, vd), BF16),
                        pltpu.VMEM((vd + BF16_SUBLANES, S), BF16)],
        compiler_params=_params("parallel", "parallel", "arbitrary"),
        name="diff_attention",
    )(lam, qkv, qkv, qkv, bias_tiles, subln.reshape(1, vd))


def _dft_tables(n, scale, rows=None):
    j = (jnp.arange(n, dtype=jnp.int32) if rows is None else rows)[:, None]

    def trig(k):
        ang = ((j * k[None, :]) % n).astype(F32) * (2.0 * math.pi / n)
        return jnp.cos(ang), jnp.sin(ang)

    w = min(LANES, n)
    c0, s0 = (jnp.tile(t, (1, n // w)) for t in trig(jnp.arange(w, dtype=jnp.int32)))
    c1, s1 = (jnp.repeat(t, w, axis=1) for t in trig(jnp.arange(n // w, dtype=jnp.int32) * w))
    return ((c1 * c0 - s1 * s0) * scale).astype(BF16), ((s1 * c0 + c1 * s0) * scale).astype(BF16)


def _dft_seq_tables(S, tm):
    nv = S // (2 * tm)
    rows = (jnp.arange(nv, dtype=jnp.int32)[:, None] * tm
            + jnp.arange(tm + BF16_SUBLANES, dtype=jnp.int32)[None, :]).reshape(-1)
    return _dft_tables(S, S ** -0.5, rows)


def _dft_seq_kernel(cs_ref, ss_ref, a_ref, b_ref, top_ref, bot_ref):
    tm = top_ref.shape[0]
    p = jnp.dot(cs_ref[...], a_ref[...], preferred_element_type=F32)
    q = jnp.dot(ss_ref[...], b_ref[...], preferred_element_type=F32)
    top_ref[...] = (p - q)[0:tm].astype(top_ref.dtype)
    ext = cs_ref.shape[0]
    rho = lax.broadcasted_iota(jnp.int32, (tm, ext), 0)
    col = lax.broadcasted_iota(jnp.int32, (tm, ext), 1)
    mirror = jnp.where(col == tm - rho, 1.0, 0.0).astype(BF16)
    bot_ref[...] = jnp.dot(mirror, (p + q).astype(BF16), preferred_element_type=F32).astype(bot_ref.dtype)


def _dft_seq(a, b, cos_s, sin_s, *, B, S, tm, tn):
    M, N = a.shape
    nv = S // (2 * tm)
    ext = tm + BF16_SUBLANES
    assert S % (2 * tm) == 0 and cos_s.shape == (nv * ext, S)
    half = jax.ShapeDtypeStruct((M // 2, N), BF16)
    return pl.pallas_call(
        _dft_seq_kernel,
        grid=(B, N // tn, nv),
        in_specs=[pl.BlockSpec((ext, S), lambda bb, n, v: (v, 0)),
                  pl.BlockSpec((ext, S), lambda bb, n, v: (v, 0)),
                  pl.BlockSpec((S, tn), lambda bb, n, v: (bb, n)),
                  pl.BlockSpec((S, tn), lambda bb, n, v: (bb, n))],
        out_specs=[pl.BlockSpec((tm, tn), lambda bb, n, v: (bb * nv + v, n)),
                   pl.BlockSpec((tm, tn), lambda bb, n, v: (bb * nv + nv - 1 - v, n))],
        out_shape=[half, half],
        compiler_params=_params("parallel", "parallel", "arbitrary"),
        name="dft_seq",
    )(cos_s, sin_s, a, b)


def _out_proj_kernel(o_ref, ftop_ref, fbot_ref, wa_ref, wf_ref, g_ref, x_ref, y_ref, *, tiles_per_seq):
    first_half = (pl.program_id(0) % tiles_per_seq) < tiles_per_seq // 2
    f = jnp.where(first_half, ftop_ref[...], fbot_ref[...])
    m = (jnp.dot(o_ref[...], wa_ref[...], preferred_element_type=F32)
         + jnp.dot(f, wf_ref[...], preferred_element_type=F32))
    y_ref[...] = x_ref[...] + _rms(m, g_ref[...])


def _out_proj(o, f_top, f_bot, w_out, layer, g, x, seq_len, *, tm):
    M, D = x.shape
    ka, kf = o.shape[1], f_top.shape[1]
    tps = seq_len // tm
    assert ka % kf == 0 and w_out.shape[1] == ka + kf and tps % 2 == 0
    half_tile = lambda i: ((i // tps) * (tps // 2) + i % (tps // 2), 0)
    return pl.pallas_call(
        functools.partial(_out_proj_kernel, tiles_per_seq=tps),
        grid=(M // tm,),
        in_specs=[pl.BlockSpec((tm, ka), lambda i: (i, 0)),
                  pl.BlockSpec((tm, kf), half_tile),
                  pl.BlockSpec((tm, kf), half_tile),
                  pl.BlockSpec((None, ka, D), lambda i: (layer, 0, 0)),
                  pl.BlockSpec((None, kf, D), lambda i: (layer, ka // kf, 0)),
                  pl.BlockSpec((1, D), lambda i: (0, 0)),
                  pl.BlockSpec((tm, D), lambda i: (i, 0))],
        out_specs=pl.BlockSpec((tm, D), lambda i: (i, 0)),
        out_shape=jax.ShapeDtypeStruct((M, D), F32),
        compiler_params=_params("parallel"),
        name="out_proj",
    )(o, f_top, f_bot, w_out, w_out, g.reshape(1, D), x)


def _tiles(seq_len):
    return dict(tm_block=min(512, seq_len), tf_block=512, tm_proj=min(1024, seq_len), tn_proj=1024,
                tm_out=min(512, seq_len // 2), tm_dft=min(512, seq_len // 2), tn_dft=512,
                attn_q=min(ATTN_QUERIES, seq_len), attn_k=min(ATTN_KEY_BLOCK, seq_len))


def _run_trunk(x3, p, consts):
    B, S, D = x3.shape
    x = x3.reshape(B * S, D)
    t = _tiles(S)
    depth = p["norm_pre_mix"].shape[0]
    H = p["rel_bias"].shape[1]
    head_dim = p["ab_lambda_q1"].shape[1]
    attn_width = H * 2 * head_dim
    cos_c, sin_c = consts["dft_channel"]
    cs_c = jnp.concatenate([cos_c, sin_c], axis=1)
    cos_s, sin_s = _dft_seq_tables(S, t["tm_dft"])
    for i in range(depth):
        j = i // 2
        if i % 2 == 0:
            lam_init = _lambda_init(i)
            qkv, fa, fb = _norm_in_proj(x, p["norm_pre_mix"][i], p["ab_w_in"], j, 3 * attn_width, cs_c,
                                        tm=t["tm_proj"], tn=t["tn_proj"])
            lam = (consts["lam"][j, 0] + lam_init).reshape(1, 1)
            o = _diff_attention(qkv, lam, consts["bias_tiles"], p["ab_subln"][j], B=B, S=S, H=H,
                                head_dim=head_dim, out_scale=1.0 - lam_init, T=t["attn_q"],
                                key_block=t["attn_k"])
            f_top, f_bot = _dft_seq(fa, fb, cos_s, sin_s, B=B, S=S, tm=t["tm_dft"], tn=t["tn_dft"])
            x = _out_proj(o, f_top, f_bot, p["ab_w_out"], j, p["norm_post_mix"][i], x, S, tm=t["tm_out"])
        else:
            x = _gated_block(x, S, p["norm_pre_mix"][i], p["norm_post_mix"][i], [p["c_w_in"]],
                             p["c_conv"][j], None, p["c_w_out"], j, mode="conv",
                             tm=t["tm_block"], tf=t["tf_block"])
        x = _gated_block(x, S, p["norm_pre_ffn"][i], p["norm_post_ffn"][i],
                         [p["ffn_w_gate"], p["ffn_w_up"]], p["ffn_conv"][i], p["ffn_conv_b"][i],
                         p["ffn_w_down"], i, mode="ffn", tm=t["tm_block"], tf=t["tf_block"])
    return x.reshape(B, S, D)


def kernel(x_prompt, x_sample, rel_bias, norm_pre_mix, norm_post_mix, norm_pre_ffn, norm_post_ffn, ab_w_in, ab_w_out, ab_lambda_q1, ab_lambda_k1, ab_lambda_q2, ab_lambda_k2, ab_subln, c_w_in, c_conv, c_w_out, ffn_w_gate, ffn_w_up, ffn_conv, ffn_conv_b, ffn_w_down):
    p = dict(rel_bias=rel_bias, norm_pre_mix=norm_pre_mix, norm_post_mix=norm_post_mix,
             norm_pre_ffn=norm_pre_ffn, norm_post_ffn=norm_post_ffn,
             ab_w_in=ab_w_in.astype(BF16), ab_w_out=ab_w_out.astype(BF16),
             ab_lambda_q1=ab_lambda_q1, ab_subln=ab_subln,
             c_w_in=c_w_in.astype(BF16), c_conv=c_conv, c_w_out=c_w_out.astype(BF16),
             ffn_w_gate=ffn_w_gate.astype(BF16), ffn_w_up=ffn_w_up.astype(BF16),
             ffn_conv=ffn_conv, ffn_conv_b=ffn_conv_b, ffn_w_down=ffn_w_down.astype(BF16))
    H = rel_bias.shape[1]
    attn_width = H * 2 * ab_lambda_q1.shape[1]
    fgroup = (x_prompt.shape[-1] - attn_width) // N_FGROUPS
    bias_tiles, lam = _attn_consts(rel_bias, ab_lambda_q1, ab_lambda_k1, ab_lambda_q2, ab_lambda_k2,
                                   T=ATTN_TILE)
    consts = dict(bias_tiles=bias_tiles, lam=lam,
                  dft_channel=_dft_tables(fgroup, fgroup ** -0.5))
    return (_run_trunk(x_prompt, p, consts), _run_trunk(x_sample, p, consts))
```

```python
import functools
import math

import jax
import jax.numpy as jnp
from jax import lax
from jax.experimental import pallas as pl
from jax.experimental.pallas import tpu as pltpu

F32 = jnp.float32
BF16 = jnp.bfloat16

EPS = 1e-6
MAX_DISTANCE = 128
N_FGROUPS = 4
V7X_VMEM_LIMIT_BYTES = 60000 * 1024
HALO_ROWS = 8
ATTN_TILE = 256
ATTN_QUERIES = 512
ATTN_KEY_BLOCK = 512
NEG_BIG = -1e30
LANES = 128
NORM_ROWS = 16
NORM_UNROLL = 8
BF16_SUBLANES = 16
LOG2_E = math.log2(math.e)


def _params(*semantics):
    return pltpu.CompilerParams(dimension_semantics=semantics,
                                vmem_limit_bytes=V7X_VMEM_LIMIT_BYTES)


def _rms(x, g):
    ms = jnp.mean(x * x, axis=-1, keepdims=True)
    return (x * lax.rsqrt(ms + EPS)) * g


def _norm_rows(src_ref, g_ref, n_rows, store):
    def body(i, carry):
        rows = pl.ds(pl.multiple_of(i * NORM_ROWS, NORM_ROWS), NORM_ROWS)
        store(rows, _rms(src_ref[rows, :], g_ref[...]))
        return carry

    lax.fori_loop(0, n_rows // NORM_ROWS, body, 0, unroll=NORM_UNROLL)


def _lambda_init(layer_idx):
    return 0.8 - 0.6 * math.exp(-0.3 * layer_idx)


def _conv_rows(t, cw, tm):
    n = t.shape[0]
    prev = pltpu.roll(t, 1, 0)[:tm]
    nxt = pltpu.roll(t, n - 1, 0)[:tm]
    return prev * cw[0:1] + t[:tm] * cw[1:2] + nxt * cw[2:3]


def _gated_block_kernel(xp_ref, x_ref, xn_ref, xp2_ref, x2_ref, xn2_ref, gpre_ref, *rest,
                        mode, tm, tiles_per_seq):
    if mode == "ffn":
        wg_ref, wu_ref, cw_ref, cb_ref, wd_ref, gpost_ref, o_ref, h_scr, acc_scr = rest
    else:
        wb_ref, wc_ref, wx_ref, cw_ref, wd_ref, gpost_ref, o_ref, h_scr, acc_scr = rest
    i = pl.program_id(0)
    j = pl.program_id(1)
    nj = pl.num_programs(1)
    slot = i % 2

    def prenorm(main_ref, prev_ref, next_ref, tile, dst):
        g = gpre_ref[...]
        it = tile % tiles_per_seq
        keep_prev = jnp.where(it != 0, 1.0, 0.0)
        keep_next = jnp.where(it != tiles_per_seq - 1, 1.0, 0.0)
        halo = jnp.concatenate([_rms(next_ref[...], g) * keep_next,
                                _rms(prev_ref[...], g) * keep_prev], axis=0)
        h_scr[dst, 0:tm, :] = _rms(main_ref[...], g).astype(BF16)
        h_scr[dst, tm:tm + 2 * HALO_ROWS, :] = halo.astype(BF16)

    @pl.when((i == 0) & (j == 0))
    def _():
        prenorm(x_ref, xp_ref, xn_ref, i, 0)

    def chunk_step():
        h_all = h_scr[slot]
        h_main = h_scr[slot, 0:tm, :]
        cw = cw_ref[...]
        if mode == "ffn":
            gate = jnp.dot(h_all, wg_ref[...], preferred_element_type=F32)
            up = jnp.dot(h_main, wu_ref[...], preferred_element_type=F32)
            g = _conv_rows(gate, cw, tm) + cb_ref[...]
            a = (g * jax.nn.sigmoid(g)) * up
        else:
            cg = jnp.dot(h_all, wc_ref[...], preferred_element_type=F32)
            xv = jnp.dot(h_all, wx_ref[...], preferred_element_type=F32)
            bg = jnp.dot(h_main, wb_ref[...], preferred_element_type=F32)
            a = _conv_rows(cg * xv, cw, tm) * bg
        return jnp.dot(a.astype(BF16), wd_ref[...], preferred_element_type=F32)

    @pl.when(j == 0)
    def _():
        acc_scr[...] = chunk_step()

    @pl.when((j > 0) & (j < nj - 1))
    def _():
        acc_scr[...] += chunk_step()

    @pl.when(j == nj - 1)
    def _():
        d = chunk_step()
        prenorm(x2_ref, xp2_ref, xn2_ref, i + 1, 1 - slot)
        acc_scr[...] += d

        def store_o(rows, y):
            o_ref[rows, :] = x_ref[rows, :] + y

        _norm_rows(acc_scr, gpost_ref, tm, store_o)


def _gated_block(x, seq_len, gpre, gpost, w_in_list, conv_w, conv_b, w_out, layer, *, mode, tm, tf):
    M, D = x.shape
    C = w_out.shape[1]
    assert M % tm == 0 and seq_len % tm == 0 and C % tf == 0 and tm % (2 * HALO_ROWS) == 0
    ni, nj = M // tm, C // tf
    tiles_per_seq = seq_len // tm
    blocks_per_tile = tm // HALO_ROWS
    last_block = M // HALO_ROWS - 1

    assert nj >= 2
    row = lambda i, j: (i, 0)

    def tile_specs(shift):
        tile = lambda i: jnp.minimum(i + shift, ni - 1)
        return [pl.BlockSpec((HALO_ROWS, D), lambda i, j: (jnp.maximum(tile(i) * blocks_per_tile - 1, 0), 0)),
                pl.BlockSpec((tm, D), lambda i, j: (tile(i), 0)),
                pl.BlockSpec((HALO_ROWS, D),
                             lambda i, j: (jnp.minimum((tile(i) + 1) * blocks_per_tile, last_block), 0))]

    in_specs = tile_specs(0) + tile_specs(1) + [pl.BlockSpec((1, D), lambda i, j: (0, 0))]
    args = [x, x, x, x, x, x, gpre.reshape(1, D)]
    if mode == "ffn":
        wg, wu = w_in_list
        in_specs += [pl.BlockSpec((None, D, tf), lambda i, j: (layer, 0, j)),
                     pl.BlockSpec((None, D, tf), lambda i, j: (layer, 0, j)),
                     pl.BlockSpec((3, tf), lambda i, j: (0, j)),
                     pl.BlockSpec((1, tf), lambda i, j: (0, j))]
        args += [wg, wu, conv_w, conv_b.reshape(1, C)]
    else:
        (w_in,) = w_in_list
        in_specs += [pl.BlockSpec((None, D, tf), lambda i, j: (layer, 0, j)),
                     pl.BlockSpec((None, D, tf), lambda i, j: (layer, 0, nj + j)),
                     pl.BlockSpec((None, D, tf), lambda i, j: (layer, 0, 2 * nj + j)),
                     pl.BlockSpec((3, tf), lambda i, j: (0, j))]
        args += [w_in, w_in, w_in, conv_w]
    in_specs += [pl.BlockSpec((None, tf, D), lambda i, j: (layer, j, 0)),
                 pl.BlockSpec((1, D), lambda i, j: (0, 0))]
    args += [w_out, gpost.reshape(1, D)]

    return pl.pallas_call(
        functools.partial(_gated_block_kernel, mode=mode, tm=tm, tiles_per_seq=tiles_per_seq),
        grid=(ni, nj),
        in_specs=in_specs,
        out_specs=pl.BlockSpec((tm, D), row),
        out_shape=jax.ShapeDtypeStruct((M, D), F32),
        scratch_shapes=[pltpu.VMEM((2, tm + 2 * HALO_ROWS, D), BF16),
                        pltpu.VMEM((tm, D), F32)],
        compiler_params=_params("arbitrary", "arbitrary"),
        name="gated_block_" + mode,
    )(*args)


def _norm_in_proj_kernel(x_ref, g_ref, w_ref, cs_ref, qkv_ref, fa_ref, fb_ref, h_scr, *, n_qkv_tiles):
    j = pl.program_id(1)

    @pl.when(j == 0)
    def _():
        def store_h(rows, y):
            h_scr[rows, :] = y.astype(BF16)

        _norm_rows(x_ref, g_ref, h_scr.shape[0], store_h)

    y = jnp.dot(h_scr[...], w_ref[...], preferred_element_type=F32)

    @pl.when(j < n_qkv_tiles)
    def _():
        for c in range(qkv_ref.shape[0]):
            qkv_ref[c] = y[:, c * LANES:(c + 1) * LANES].astype(qkv_ref.dtype)

    @pl.when(j >= n_qkv_tiles)
    def _():
        c = cs_ref.shape[0]
        f = y.astype(BF16)
        for grp in range(N_FGROUPS):
            cols = slice(grp * c, (grp + 1) * c)
            ab = jnp.dot(f[:, cols], cs_ref[...], preferred_element_type=F32)
            fa_ref[:, cols] = ab[:, :c].astype(fa_ref.dtype)
            fb_ref[:, cols] = ab[:, c:].astype(fb_ref.dtype)


def _norm_in_proj(x, g, w, layer, qkv_width, cs, *, tm, tn):
    M, D = x.shape
    N = w.shape[2]
    fw = N - qkv_width
    assert M % tm == 0 and qkv_width % tn == 0 and fw == tn and tn % LANES == 0
    assert cs.shape == (fw // N_FGROUPS, 2 * fw // N_FGROUPS)
    n_qkv_tiles = qkv_width // tn
    cols = tn // LANES
    f_out = jax.ShapeDtypeStruct((M, fw), BF16)
    f_spec = pl.BlockSpec((tm, fw), lambda i, j: (i, 0))
    return pl.pallas_call(
        functools.partial(_norm_in_proj_kernel, n_qkv_tiles=n_qkv_tiles),
        grid=(M // tm, N // tn),
        in_specs=[pl.BlockSpec((tm, D), lambda i, j: (i, 0)),
                  pl.BlockSpec((1, D), lambda i, j: (0, 0)),
                  pl.BlockSpec((None, D, tn), lambda i, j: (layer, 0, j)),
                  pl.BlockSpec(cs.shape, lambda i, j: (0, 0))],
        out_specs=[pl.BlockSpec((cols, tm, LANES), lambda i, j: (jnp.minimum(j, n_qkv_tiles - 1), i, 0)),
                   f_spec, f_spec],
        out_shape=[jax.ShapeDtypeStruct((qkv_width // LANES, M, LANES), BF16), f_out, f_out],
        scratch_shapes=[pltpu.VMEM((tm, D), BF16)],
        compiler_params=_params("parallel", "arbitrary"),
        name="norm_in_proj",
    )(x, g.reshape(1, D), w, cs)


def _bucket_saturation_distance(num_buckets):
    nb = num_buckets // 2
    max_exact = nb // 2
    n = max_exact
    while max_exact + int(math.log(n / max_exact) / math.log(MAX_DISTANCE / max_exact)
                          * (nb - max_exact)) < nb - 1:
        n += 1
    return n


def _attn_consts_kernel(rb_ref, lq1_ref, lk1_ref, lq2_ref, lk2_ref, bias_ref, lam_ref, *, T, num_buckets):
    h = pl.program_id(0)
    nb = num_buckets // 2
    max_exact = nb // 2
    row = lax.broadcasted_iota(jnp.int32, (T, T), 0)
    col = lax.broadcasted_iota(jnp.int32, (T, T), 1)
    for t, off in enumerate((-2, -1, 0, 1, 2)):
        rel = row - col + off * T
        ret = jnp.where(rel > 0, nb, 0)
        n = jnp.abs(rel)
        nf = jnp.maximum(n, 1).astype(F32)
        large = max_exact + (jnp.log(nf / max_exact) / math.log(MAX_DISTANCE / max_exact)
                             * (nb - max_exact)).astype(jnp.int32)
        large = jnp.minimum(large, nb - 1)
        bucket = ret + jnp.where(n < max_exact, n, large)
        bias = jnp.zeros((T, T), F32)
        for b in range(num_buckets):
            bias = jnp.where(bucket == b, rb_ref[b, h], bias)
        bias_ref[0, t] = bias * LOG2_E
    s1 = jnp.sum(lq1_ref[...] * lk1_ref[...], axis=-1, keepdims=True)
    s2 = jnp.sum(lq2_ref[...] * lk2_ref[...], axis=-1, keepdims=True)
    lam_ref[...] = jnp.broadcast_to(jnp.exp(s1) - jnp.exp(s2), lam_ref.shape)


def _attn_consts(rel_bias, lq1, lk1, lq2, lk2, *, T):
    num_buckets, H = rel_bias.shape
    n_ab = lq1.shape[0]
    assert T + 1 >= _bucket_saturation_distance(num_buckets)
    vec = pl.BlockSpec(lq1.shape, lambda h: (0, 0))
    return pl.pallas_call(
        functools.partial(_attn_consts_kernel, T=T, num_buckets=num_buckets),
        grid=(H,),
        in_specs=[pl.BlockSpec(memory_space=pltpu.SMEM), vec, vec, vec, vec],
        out_specs=[pl.BlockSpec((1, 5, T, T), lambda h: (h, 0, 0, 0)),
                   pl.BlockSpec((n_ab, 128), lambda h: (0, 0))],
        out_shape=[jax.ShapeDtypeStruct((H, 5, T, T), F32),
                   jax.ShapeDtypeStruct((n_ab, 128), F32)],
        compiler_params=_params("arbitrary"),
        name="attn_consts",
    )(rel_bias, lq1, lk1, lq2, lk2)


def _attn_kernel(lam_ref, q_ref, k_ref, v_ref, bias_ref, sub_ref, o_ref,
                 qm_scr, vt_scr, *, T, bt, key_block, S, head_dim, out_scale):
    qi = pl.program_id(2)
    vd = v_ref.shape[1]

    @pl.when(qi == 0)
    def _():
        vt_scr[0:vd, :] = v_ref[...].T
        vt_scr[vd:, :] = jnp.ones((vt_scr.shape[0] - vd, vt_scr.shape[1]), BF16)

    q = q_ref[...].astype(F32) * (head_dim ** -0.5 * LOG2_E)
    lane = lax.broadcasted_iota(jnp.int32, q.shape, 1)
    qm_scr[0:T, :] = jnp.where(lane < head_dim, q, 0.0).astype(BF16)
    qm_scr[T:2 * T, :] = jnp.where(lane >= head_dim, q, 0.0).astype(BF16)
    qm = qm_scr[...]
    n_blocks = S // key_block
    key_tiles, query_tiles = key_block // bt, T // bt

    def keys_of(j):
        return slice(j * key_block, (j + 1) * key_block)

    def scores(j):
        bias = jnp.concatenate(
            [jnp.concatenate([bias_ref[0, jnp.clip(j * key_tiles + kc - (qi * query_tiles + qc), -2, 2) + 2]
                              for kc in range(key_tiles)], axis=0)
             for qc in range(query_tiles)], axis=1)
        t = lax.dot_general(k_ref[keys_of(j), :], qm, (((1,), (1,)), ((), ())),
                            preferred_element_type=F32) + jnp.concatenate([bias, bias], axis=1)
        return t, jnp.max(t, axis=0, keepdims=True)

    def weighted_values(j, e):
        return jnp.dot(vt_scr[:, keys_of(j)], e, preferred_element_type=F32)

    m = jnp.full((1, 2 * T), NEG_BIG, F32)
    acc = jnp.zeros((vt_scr.shape[0], 2 * T), F32)
    t_next, mx_next = scores(0)
    e_prev = alpha_prev = None
    for j in range(n_blocks):
        t, mx = t_next, mx_next
        if j + 1 < n_blocks:
            t_next, mx_next = scores(j + 1)
        m_new = jnp.maximum(m, mx)
        alpha = jnp.exp2(m - m_new)
        e = jnp.exp2(t - m_new).astype(BF16)
        if j > 0:
            acc = alpha_prev * acc + weighted_values(j - 1, e_prev)
        e_prev, alpha_prev, m = e, alpha, m_new
    acc = alpha_prev * acc + weighted_values(n_blocks - 1, e_prev)
    ot = acc[0:vd] / acc[vd:vd + 1]
    lam = lam_ref[0, 0]
    o = (ot[:, 0:T] - lam * ot[:, T:2 * T]).T
    o_ref[...] = (_rms(o, sub_ref[...]) * out_scale).astype(o_ref.dtype)


def _diff_attention(qkv, lam, bias_tiles, subln, *, B, S, H, head_dim, out_scale, T, key_block):
    M = qkv.shape[1]
    vd = 2 * head_dim
    bt = bias_tiles.shape[-1]
    assert vd == LANES and S % T == 0 and S % key_block == 0 and qkv.shape[0] == 3 * H
    assert T % bt == 0 and key_block % bt == 0
    nq = S // T
    return pl.pallas_call(
        functools.partial(_attn_kernel, T=T, bt=bt, key_block=key_block, S=S, head_dim=head_dim,
                          out_scale=out_scale),
        grid=(B, H, nq),
        in_specs=[pl.BlockSpec(memory_space=pltpu.SMEM),
                  pl.BlockSpec((None, T, vd), lambda b, h, i: (h, b * nq + i, 0)),
                  pl.BlockSpec((None, S, vd), lambda b, h, i: (H + h, b, 0)),
                  pl.BlockSpec((None, S, vd), lambda b, h, i: (2 * H + h, b, 0)),
                  pl.BlockSpec((1, 5, bt, bt), lambda b, h, i: (h, 0, 0, 0)),
                  pl.BlockSpec((1, vd), lambda b, h, i: (0, 0))],
        out_specs=pl.BlockSpec((T, vd), lambda b, h, i: (b * nq + i, h)),
        out_shape=jax.ShapeDtypeStruct((M, H * vd), BF16),
        scratch_shapes=[pltpu.VMEM((2 * T, vd), BF16),
                        pltpu.VMEM((vd + BF16_SUBLANES, S), BF16)],
        compiler_params=_params("parallel", "parallel", "arbitrary"),
        name="diff_attention",
    )(lam, qkv, qkv, qkv, bias_tiles, subln.reshape(1, vd))


def _dft_tables(n, scale, rows=None):
    j = (jnp.arange(n, dtype=jnp.int32) if rows is None else rows)[:, None]

    def trig(k):
        ang = ((j * k[None, :]) % n).astype(F32) * (2.0 * math.pi / n)
        return jnp.cos(ang), jnp.sin(ang)

    w = min(LANES, n)
    c0, s0 = (jnp.tile(t, (1, n // w)) for t in trig(jnp.arange(w, dtype=jnp.int32)))
    c1, s1 = (jnp.repeat(t, w, axis=1) for t in trig(jnp.arange(n // w, dtype=jnp.int32) * w))
    return ((c1 * c0 - s1 * s0) * scale).astype(BF16), ((s1 * c0 + c1 * s0) * scale).astype(BF16)


def _dft_seq_tables(S, tm):
    nv = S // (2 * tm)
    rows = (jnp.arange(nv, dtype=jnp.int32)[:, None] * tm
            + jnp.arange(tm + BF16_SUBLANES, dtype=jnp.int32)[None, :]).reshape(-1)
    return _dft_tables(S, S ** -0.5, rows)


def _dft_seq_kernel(cs_ref, ss_ref, a_ref, b_ref, top_ref, bot_ref):
    tm = top_ref.shape[0]
    p = jnp.dot(cs_ref[...], a_ref[...], preferred_element_type=F32)
    q = jnp.dot(ss_ref[...], b_ref[...], preferred_element_type=F32)
    top_ref[...] = (p - q)[0:tm].astype(top_ref.dtype)
    ext = cs_ref.shape[0]
    rho = lax.broadcasted_iota(jnp.int32, (tm, ext), 0)
    col = lax.broadcasted_iota(jnp.int32, (tm, ext), 1)
    mirror = jnp.where(col == tm - rho, 1.0, 0.0).astype(BF16)
    bot_ref[...] = jnp.dot(mirror, (p + q).astype(BF16), preferred_element_type=F32).astype(bot_ref.dtype)


def _dft_seq(a, b, cos_s, sin_s, *, B, S, tm, tn):
    M, N = a.shape
    nv = S // (2 * tm)
    ext = tm + BF16_SUBLANES
    assert S % (2 * tm) == 0 and cos_s.shape == (nv * ext, S)
    half = jax.ShapeDtypeStruct((M // 2, N), BF16)
    return pl.pallas_call(
        _dft_seq_kernel,
        grid=(B, N // tn, nv),
        in_specs=[pl.BlockSpec((ext, S), lambda bb, n, v: (v, 0)),
                  pl.BlockSpec((ext, S), lambda bb, n, v: (v, 0)),
                  pl.BlockSpec((S, tn), lambda bb, n, v: (bb, n)),
                  pl.BlockSpec((S, tn), lambda bb, n, v: (bb, n))],
        out_specs=[pl.BlockSpec((tm, tn), lambda bb, n, v: (bb * nv + v, n)),
                   pl.BlockSpec((tm, tn), lambda bb, n, v: (bb * nv + nv - 1 - v, n))],
        out_shape=[half, half],
        compiler_params=_params("parallel", "parallel", "arbitrary"),
        name="dft_seq",
    )(cos_s, sin_s, a, b)


def _out_proj_kernel(o_ref, ftop_ref, fbot_ref, wa_ref, wf_ref, g_ref, x_ref, y_ref, *, tiles_per_seq):
    first_half = (pl.program_id(0) % tiles_per_seq) < tiles_per_seq // 2
    f = jnp.where(first_half, ftop_ref[...], fbot_ref[...])
    m = (jnp.dot(o_ref[...], wa_ref[...], preferred_element_type=F32)
         + jnp.dot(f, wf_ref[...], preferred_element_type=F32))
    y_ref[...] = x_ref[...] + _rms(m, g_ref[...])


def _out_proj(o, f_top, f_bot, w_out, layer, g, x, seq_len, *, tm):
    M, D = x.shape
    ka, kf = o.shape[1], f_top.shape[1]
    tps = seq_len // tm
    assert ka % kf == 0 and w_out.shape[1] == ka + kf and tps % 2 == 0
    half_tile = lambda i: ((i // tps) * (tps // 2) + i % (tps // 2), 0)
    return pl.pallas_call(
        functools.partial(_out_proj_kernel, tiles_per_seq=tps),
        grid=(M // tm,),
        in_specs=[pl.BlockSpec((tm, ka), lambda i: (i, 0)),
                  pl.BlockSpec((tm, kf), half_tile),
                  pl.BlockSpec((tm, kf), half_tile),
                  pl.BlockSpec((None, ka, D), lambda i: (layer, 0, 0)),
                  pl.BlockSpec((None, kf, D), lambda i: (layer, ka // kf, 0)),
                  pl.BlockSpec((1, D), lambda i: (0, 0)),
                  pl.BlockSpec((tm, D), lambda i: (i, 0))],
        out_specs=pl.BlockSpec((tm, D), lambda i: (i, 0)),
        out_shape=jax.ShapeDtypeStruct((M, D), F32),
        compiler_params=_params("parallel"),
        name="out_proj",
    )(o, f_top, f_bot, w_out, w_out, g.reshape(1, D), x)


def _tiles(seq_len):
    return dict(tm_block=min(512, seq_len), tf_block=512, tm_proj=min(1024, seq_len), tn_proj=1024,
                tm_out=min(512, seq_len // 2), tm_dft=min(512, seq_len // 2), tn_dft=512,
                attn_q=min(ATTN_QUERIES, seq_len), attn_k=min(ATTN_KEY_BLOCK, seq_len))


def _run_trunk(x3, p, consts):
    B, S, D = x3.shape
    x = x3.reshape(B * S, D)
    t = _tiles(S)
    depth = p["norm_pre_mix"].shape[0]
    H = p["rel_bias"].shape[1]
    head_dim = p["ab_lambda_q1"].shape[1]
    attn_width = H * 2 * head_dim
    cos_c, sin_c = consts["dft_channel"]
    cs_c = jnp.concatenate([cos_c, sin_c], axis=1)
    cos_s, sin_s = _dft_seq_tables(S, t["tm_dft"])
    for i in range(depth):
        j = i // 2
        if i % 2 == 0:
            lam_init = _lambda_init(i)
            qkv, fa, fb = _norm_in_proj(x, p["norm_pre_mix"][i], p["ab_w_in"], j, 3 * attn_width, cs_c,
                                        tm=t["tm_proj"], tn=t["tn_proj"])
            lam = (consts["lam"][j, 0] + lam_init).reshape(1, 1)
            o = _diff_attention(qkv, lam, consts["bias_tiles"], p["ab_subln"][j], B=B, S=S, H=H,
                                head_dim=head_dim, out_scale=1.0 - lam_init, T=t["attn_q"],
                                key_block=t["attn_k"])
            f_top, f_bot = _dft_seq(fa, fb, cos_s, sin_s, B=B, S=S, tm=t["tm_dft"], tn=t["tn_dft"])
            x = _out_proj(o, f_top, f_bot, p["ab_w_out"], j, p["norm_post_mix"][i], x, S, tm=t["tm_out"])
        else:
            x = _gated_block(x, S, p["norm_pre_mix"][i], p["norm_post_mix"][i], [p["c_w_in"]],
                             p["c_conv"][j], None, p["c_w_out"], j, mode="conv",
                             tm=t["tm_block"], tf=t["tf_block"])
        x = _gated_block(x, S, p["norm_pre_ffn"][i], p["norm_post_ffn"][i],
                         [p["ffn_w_gate"], p["ffn_w_up"]], p["ffn_conv"][i], p["ffn_conv_b"][i],
                         p["ffn_w_down"], i, mode="ffn", tm=t["tm_block"], tf=t["tf_block"])
    return x.reshape(B, S, D)


def kernel(x_prompt, x_sample, rel_bias, norm_pre_mix, norm_post_mix, norm_pre_ffn, norm_post_ffn, ab_w_in, ab_w_out, ab_lambda_q1, ab_lambda_k1, ab_lambda_q2, ab_lambda_k2, ab_subln, c_w_in, c_conv, c_w_out, ffn_w_gate, ffn_w_up, ffn_conv, ffn_conv_b, ffn_w_down):
    p = dict(rel_bias=rel_bias, norm_pre_mix=norm_pre_mix, norm_post_mix=norm_post_mix,
             norm_pre_ffn=norm_pre_ffn, norm_post_ffn=norm_post_ffn,
             ab_w_in=ab_w_in.astype(BF16), ab_w_out=ab_w_out.astype(BF16),
             ab_lambda_q1=ab_lambda_q1, ab_subln=ab_subln,
             c_w_in=c_w_in.astype(BF16), c_conv=c_conv, c_w_out=c_w_out.astype(BF16),
             ffn_w_gate=ffn_w_gate.astype(BF16), ffn_w_up=ffn_w_up.astype(BF16),
             ffn_conv=ffn_conv, ffn_conv_b=ffn_conv_b, ffn_w_down=ffn_w_down.astype(BF16))
    H = rel_bias.shape[1]
    attn_width = H * 2 * ab_lambda_q1.shape[1]
    fgroup = (x_prompt.shape[-1] - attn_width) // N_FGROUPS
    bias_tiles, lam = _attn_consts(rel_bias, ab_lambda_q1, ab_lambda_k1, ab_lambda_q2, ab_lambda_k2,
                                   T=ATTN_TILE)
    consts = dict(bias_tiles=bias_tiles, lam=lam,
                  dft_channel=_dft_tables(fgroup, fgroup ** -0.5))
    return (_run_trunk(x_prompt, p, consts), _run_trunk(x_sample, p, consts))
```

```python
import functools
import math

import jax
import jax.numpy as jnp
from jax import lax
from jax.experimental import pallas as pl
from jax.experimental.pallas import tpu as pltpu

F32 = jnp.float32
BF16 = jnp.bfloat16

EPS = 1e-6
MAX_DISTANCE = 128
N_FGROUPS = 4
V7X_VMEM_LIMIT_BYTES = 60000 * 1024
HALO_ROWS = 8
ATTN_TILE = 256
ATTN_QUERIES = 2048
ATTN_KEY_BLOCK = 512
NEG_BIG = -1e30
LANES = 128
NORM_ROWS = 16
NORM_UNROLL = 8
BF16_SUBLANES = 16
LOG2_E = math.log2(math.e)


def _params(*semantics):
    return pltpu.CompilerParams(dimension_semantics=semantics,
                                vmem_limit_bytes=V7X_VMEM_LIMIT_BYTES)


def _rms(x, g):
    ms = jnp.mean(x * x, axis=-1, keepdims=True)
    return (x * lax.rsqrt(ms + EPS)) * g


def _norm_rows(src_ref, g_ref, n_rows, store):
    def body(i, carry):
        rows = pl.ds(pl.multiple_of(i * NORM_ROWS, NORM_ROWS), NORM_ROWS)
        store(rows, _rms(src_ref[rows, :], g_ref[...]))
        return carry

    lax.fori_loop(0, n_rows // NORM_ROWS, body, 0, unroll=NORM_UNROLL)


def _lambda_init(layer_idx):
    return 0.8 - 0.6 * math.exp(-0.3 * layer_idx)


def _conv_rows(t, cw, tm):
    n = t.shape[0]
    prev = pltpu.roll(t, 1, 0)[:tm]
    nxt = pltpu.roll(t, n - 1, 0)[:tm]
    return prev * cw[0:1] + t[:tm] * cw[1:2] + nxt * cw[2:3]


def _gated_block_kernel(xp_ref, x_ref, xn_ref, gpre_ref, *rest, mode, tm, tiles_per_seq):
    if mode == "ffn":
        wg_ref, wu_ref, cw_ref, cb_ref, wd_ref, gpost_ref, o_ref, h_scr, acc_scr = rest
    else:
        wb_ref, wc_ref, wx_ref, cw_ref, wd_ref, gpost_ref, o_ref, h_scr, acc_scr = rest
    i = pl.program_id(0)
    j = pl.program_id(1)
    nj = pl.num_programs(1)

    @pl.when(j == 0)
    def _():
        g = gpre_ref[...]

        def store_h(rows, y):
            h_scr[rows, :] = y.astype(BF16)

        _norm_rows(x_ref, gpre_ref, tm, store_h)
        it = i % tiles_per_seq
        keep_prev = jnp.where(it != 0, 1.0, 0.0)
        keep_next = jnp.where(it != tiles_per_seq - 1, 1.0, 0.0)
        halo = jnp.concatenate([_rms(xn_ref[...], g) * keep_next,
                                _rms(xp_ref[...], g) * keep_prev], axis=0)
        h_scr[tm:tm + 2 * HALO_ROWS, :] = halo.astype(BF16)
        acc_scr[...] = jnp.zeros_like(acc_scr)

    h_all = h_scr[...]
    h_main = h_scr[0:tm, :]
    cw = cw_ref[...]
    if mode == "ffn":
        gate = jnp.dot(h_all, wg_ref[...], preferred_element_type=F32)
        up = jnp.dot(h_main, wu_ref[...], preferred_element_type=F32)
        g = _conv_rows(gate, cw, tm) + cb_ref[...]
        a = (g * jax.nn.sigmoid(g)) * up
    else:
        cg = jnp.dot(h_all, wc_ref[...], preferred_element_type=F32)
        xv = jnp.dot(h_all, wx_ref[...], preferred_element_type=F32)
        bg = jnp.dot(h_main, wb_ref[...], preferred_element_type=F32)
        a = _conv_rows(cg * xv, cw, tm) * bg
    acc_scr[...] += jnp.dot(a.astype(BF16), wd_ref[...], preferred_element_type=F32)

    @pl.when(j == nj - 1)
    def _():
        def store_o(rows, y):
            o_ref[rows, :] = x_ref[rows, :] + y

        _norm_rows(acc_scr, gpost_ref, tm, store_o)


def _gated_block(x, seq_len, gpre, gpost, w_in_list, conv_w, conv_b, w_out, layer, *, mode, tm, tf):
    M, D = x.shape
    C = w_out.shape[1]
    assert M % tm == 0 and seq_len % tm == 0 and C % tf == 0 and tm % (2 * HALO_ROWS) == 0
    ni, nj = M // tm, C // tf
    tiles_per_seq = seq_len // tm
    blocks_per_tile = tm // HALO_ROWS
    last_block = M // HALO_ROWS - 1

    row = lambda i, j: (i, 0)
    in_specs = [
        pl.BlockSpec((HALO_ROWS, D), lambda i, j: (jnp.maximum(i * blocks_per_tile - 1, 0), 0)),
        pl.BlockSpec((tm, D), row),
        pl.BlockSpec((HALO_ROWS, D), lambda i, j: (jnp.minimum((i + 1) * blocks_per_tile, last_block), 0)),
        pl.BlockSpec((1, D), lambda i, j: (0, 0)),
    ]
    args = [x, x, x, gpre.reshape(1, D)]
    if mode == "ffn":
        wg, wu = w_in_list
        in_specs += [pl.BlockSpec((None, D, tf), lambda i, j: (layer, 0, j)),
                     pl.BlockSpec((None, D, tf), lambda i, j: (layer, 0, j)),
                     pl.BlockSpec((3, tf), lambda i, j: (0, j)),
                     pl.BlockSpec((1, tf), lambda i, j: (0, j))]
        args += [wg, wu, conv_w, conv_b.reshape(1, C)]
    else:
        (w_in,) = w_in_list
        in_specs += [pl.BlockSpec((None, D, tf), lambda i, j: (layer, 0, j)),
                     pl.BlockSpec((None, D, tf), lambda i, j: (layer, 0, nj + j)),
                     pl.BlockSpec((None, D, tf), lambda i, j: (layer, 0, 2 * nj + j)),
                     pl.BlockSpec((3, tf), lambda i, j: (0, j))]
        args += [w_in, w_in, w_in, conv_w]
    in_specs += [pl.BlockSpec((None, tf, D), lambda i, j: (layer, j, 0)),
                 pl.BlockSpec((1, D), lambda i, j: (0, 0))]
    args += [w_out, gpost.reshape(1, D)]

    return pl.pallas_call(
        functools.partial(_gated_block_kernel, mode=mode, tm=tm, tiles_per_seq=tiles_per_seq),
        grid=(ni, nj),
        in_specs=in_specs,
        out_specs=pl.BlockSpec((tm, D), row),
        out_shape=jax.ShapeDtypeStruct((M, D), F32),
        scratch_shapes=[pltpu.VMEM((tm + 2 * HALO_ROWS, D), BF16),
                        pltpu.VMEM((tm, D), F32)],
        compiler_params=_params("parallel", "arbitrary"),
        name="gated_block_" + mode,
    )(*args)


def _norm_in_proj_kernel(x_ref, g_ref, w_ref, cs_ref, qkv_ref, fa_ref, fb_ref, h_scr, *, n_qkv_tiles):
    j = pl.program_id(1)

    @pl.when(j == 0)
    def _():
        def store_h(rows, y):
            h_scr[rows, :] = y.astype(BF16)

        _norm_rows(x_ref, g_ref, h_scr.shape[0], store_h)

    y = jnp.dot(h_scr[...], w_ref[...], preferred_element_type=F32)

    @pl.when(j < n_qkv_tiles)
    def _():
        for c in range(qkv_ref.shape[0]):
            qkv_ref[c] = y[:, c * LANES:(c + 1) * LANES].astype(qkv_ref.dtype)

    @pl.when(j >= n_qkv_tiles)
    def _():
        c = cs_ref.shape[0]
        f = y.astype(BF16)
        for grp in range(N_FGROUPS):
            cols = slice(grp * c, (grp + 1) * c)
            ab = jnp.dot(f[:, cols], cs_ref[...], preferred_element_type=F32)
            fa_ref[:, cols] = ab[:, :c].astype(fa_ref.dtype)
            fb_ref[:, cols] = ab[:, c:].astype(fb_ref.dtype)


def _norm_in_proj(x, g, w, layer, qkv_width, cs, *, tm, tn):
    M, D = x.shape
    N = w.shape[2]
    fw = N - qkv_width
    assert M % tm == 0 and qkv_width % tn == 0 and fw == tn and tn % LANES == 0
    assert cs.shape == (fw // N_FGROUPS, 2 * fw // N_FGROUPS)
    n_qkv_tiles = qkv_width // tn
    cols = tn // LANES
    f_out = jax.ShapeDtypeStruct((M, fw), BF16)
    f_spec = pl.BlockSpec((tm, fw), lambda i, j: (i, 0))
    return pl.pallas_call(
        functools.partial(_norm_in_proj_kernel, n_qkv_tiles=n_qkv_tiles),
        grid=(M // tm, N // tn),
        in_specs=[pl.BlockSpec((tm, D), lambda i, j: (i, 0)),
                  pl.BlockSpec((1, D), lambda i, j: (0, 0)),
                  pl.BlockSpec((None, D, tn), lambda i, j: (layer, 0, j)),
                  pl.BlockSpec(cs.shape, lambda i, j: (0, 0))],
        out_specs=[pl.BlockSpec((cols, tm, LANES), lambda i, j: (jnp.minimum(j, n_qkv_tiles - 1), i, 0)),
                   f_spec, f_spec],
        out_shape=[jax.ShapeDtypeStruct((qkv_width // LANES, M, LANES), BF16), f_out, f_out],
        scratch_shapes=[pltpu.VMEM((tm, D), BF16)],
        compiler_params=_params("parallel", "arbitrary"),
        name="norm_in_proj",
    )(x, g.reshape(1, D), w, cs)


def _bucket_saturation_distance(num_buckets):
    nb = num_buckets // 2
    max_exact = nb // 2
    n = max_exact
    while max_exact + int(math.log(n / max_exact) / math.log(MAX_DISTANCE / max_exact)
                          * (nb - max_exact)) < nb - 1:
        n += 1
    return n


def _attn_consts_kernel(rb_ref, lq1_ref, lk1_ref, lq2_ref, lk2_ref, bias_ref, lam_ref, *, T, num_buckets):
    h = pl.program_id(0)
    nb = num_buckets // 2
    max_exact = nb // 2
    row = lax.broadcasted_iota(jnp.int32, (T, T), 0)
    col = lax.broadcasted_iota(jnp.int32, (T, T), 1)
    for t, off in enumerate((-2, -1, 0, 1, 2)):
        rel = row - col + off * T
        ret = jnp.where(rel > 0, nb, 0)
        n = jnp.abs(rel)
        nf = jnp.maximum(n, 1).astype(F32)
        large = max_exact + (jnp.log(nf / max_exact) / math.log(MAX_DISTANCE / max_exact)
                             * (nb - max_exact)).astype(jnp.int32)
        large = jnp.minimum(large, nb - 1)
        bucket = ret + jnp.where(n < max_exact, n, large)
        bias = jnp.zeros((T, T), F32)
        for b in range(num_buckets):
            bias = jnp.where(bucket == b, rb_ref[b, h], bias)
        bias_ref[0, t] = bias * LOG2_E
    s1 = jnp.sum(lq1_ref[...] * lk1_ref[...], axis=-1, keepdims=True)
    s2 = jnp.sum(lq2_ref[...] * lk2_ref[...], axis=-1, keepdims=True)
    lam_ref[...] = jnp.broadcast_to(jnp.exp(s1) - jnp.exp(s2), lam_ref.shape)


def _attn_consts(rel_bias, lq1, lk1, lq2, lk2, *, T):
    num_buckets, H = rel_bias.shape
    n_ab = lq1.shape[0]
    assert T + 1 >= _bucket_saturation_distance(num_buckets)
    vec = pl.BlockSpec(lq1.shape, lambda h: (0, 0))
    return pl.pallas_call(
        functools.partial(_attn_consts_kernel, T=T, num_buckets=num_buckets),
        grid=(H,),
        in_specs=[pl.BlockSpec(memory_space=pltpu.SMEM), vec, vec, vec, vec],
        out_specs=[pl.BlockSpec((1, 5, T, T), lambda h: (h, 0, 0, 0)),
                   pl.BlockSpec((n_ab, 128), lambda h: (0, 0))],
        out_shape=[jax.ShapeDtypeStruct((H, 5, T, T), F32),
                   jax.ShapeDtypeStruct((n_ab, 128), F32)],
        compiler_params=_params("arbitrary"),
        name="attn_consts",
    )(rel_bias, lq1, lk1, lq2, lk2)


def _attn_kernel(lam_ref, q_ref, k_ref, v_ref, bias_ref, sub_ref, o_ref,
                 qm_scr, vt_scr, *, T, bt, key_block, S, head_dim, out_scale):
    qi = pl.program_id(2)
    vd = v_ref.shape[1]

    @pl.when(qi == 0)
    def _():
        vt_scr[0:vd, :] = v_ref[...].T
        vt_scr[vd:, :] = jnp.ones((vt_scr.shape[0] - vd, vt_scr.shape[1]), BF16)

    q = q_ref[...].astype(F32) * (head_dim ** -0.5 * LOG2_E)
    lane = lax.broadcasted_iota(jnp.int32, q.shape, 1)
    qm_scr[0:T, :] = jnp.where(lane < head_dim, q, 0.0).astype(BF16)
    qm_scr[T:2 * T, :] = jnp.where(lane >= head_dim, q, 0.0).astype(BF16)
    qm = qm_scr[...]
    n_blocks = S // key_block
    key_tiles, query_tiles = key_block // bt, T // bt

    def keys_of(j):
        return slice(j * key_block, (j + 1) * key_block)

    def scores(j):
        bias = jnp.concatenate(
            [jnp.concatenate([bias_ref[0, jnp.clip(j * key_tiles + kc - (qi * query_tiles + qc), -2, 2) + 2]
                              for kc in range(key_tiles)], axis=0)
             for qc in range(query_tiles)], axis=1)
        t = lax.dot_general(k_ref[keys_of(j), :], qm, (((1,), (1,)), ((), ())),
                            preferred_element_type=F32) + jnp.concatenate([bias, bias], axis=1)
        return t, jnp.max(t, axis=0, keepdims=True)

    def weighted_values(j, e):
        return jnp.dot(vt_scr[:, keys_of(j)], e, preferred_element_type=F32)

    m = jnp.full((1, 2 * T), NEG_BIG, F32)
    acc = jnp.zeros((vt_scr.shape[0], 2 * T), F32)
    t_next, mx_next = scores(0)
    e_prev = alpha_prev = None
    for j in range(n_blocks):
        t, mx = t_next, mx_next
        if j + 1 < n_blocks:
            t_next, mx_next = scores(j + 1)
        m_new = jnp.maximum(m, mx)
        alpha = jnp.exp2(m - m_new)
        e = jnp.exp2(t - m_new).astype(BF16)
        if j > 0:
            acc = alpha_prev * acc + weighted_values(j - 1, e_prev)
        e_prev, alpha_prev, m = e, alpha, m_new
    acc = alpha_prev * acc + weighted_values(n_blocks - 1, e_prev)
    ot = acc[0:vd] / acc[vd:vd + 1]
    lam = lam_ref[0, 0]
    o = (ot[:, 0:T] - lam * ot[:, T:2 * T]).T
    o_ref[...] = (_rms(o, sub_ref[...]) * out_scale).astype(o_ref.dtype)


def _diff_attention(qkv, lam, bias_tiles, subln, *, B, S, H, head_dim, out_scale, T, key_block):
    M = qkv.shape[1]
    vd = 2 * head_dim
    bt = bias_tiles.shape[-1]
    assert vd == LANES and S % T == 0 and S % key_block == 0 and qkv.shape[0] == 3 * H
    assert T % bt == 0 and key_block % bt == 0
    nq = S // T
    return pl.pallas_call(
        functools.partial(_attn_kernel, T=T, bt=bt, key_block=key_block, S=S, head_dim=head_dim,
                          out_scale=out_scale),
        grid=(B, H, nq),
        in_specs=[pl.BlockSpec(memory_space=pltpu.SMEM),
                  pl.BlockSpec((None, T, vd), lambda b, h, i: (h, b * nq + i, 0)),
                  pl.BlockSpec((None, S, vd), lambda b, h, i: (H + h, b, 0)),
                  pl.BlockSpec((None, S, vd), lambda b, h, i: (2 * H + h, b, 0)),
                  pl.BlockSpec((1, 5, bt, bt), lambda b, h, i: (h, 0, 0, 0)),
                  pl.BlockSpec((1, vd), lambda b, h, i: (0, 0))],
        out_specs=pl.BlockSpec((T, vd), lambda b, h, i: (b * nq + i, h)),
        out_shape=jax.ShapeDtypeStruct((M, H * vd), BF16),
        scratch_shapes=[pltpu.VMEM((2 * T, vd), BF16),
                        pltpu.VMEM((vd + BF16_SUBLANES, S), BF16)],
        compiler_params=_params("parallel", "parallel", "arbitrary"),
        name="diff_attention",
    )(lam, qkv, qkv, qkv, bias_tiles, subln.reshape(1, vd))


def _dft_tables(n, scale, rows=None):
    j = (jnp.arange(n, dtype=jnp.int32) if rows is None else rows)[:, None]

    def trig(k):
        ang = ((j * k[None, :]) % n).astype(F32) * (2.0 * math.pi / n)
        return jnp.cos(ang), jnp.sin(ang)

    w = min(LANES, n)
    c0, s0 = (jnp.tile(t, (1, n // w)) for t in trig(jnp.arange(w, dtype=jnp.int32)))
    c1, s1 = (jnp.repeat(t, w, axis=1) for t in trig(jnp.arange(n // w, dtype=jnp.int32) * w))
    return ((c1 * c0 - s1 * s0) * scale).astype(BF16), ((s1 * c0 + c1 * s0) * scale).astype(BF16)


def _dft_seq_tables(S, tm):
    nv = S // (2 * tm)
    rows = (jnp.arange(nv, dtype=jnp.int32)[:, None] * tm
            + jnp.arange(tm + BF16_SUBLANES, dtype=jnp.int32)[None, :]).reshape(-1)
    return _dft_tables(S, S ** -0.5, rows)


def _dft_seq_kernel(cs_ref, ss_ref, a_ref, b_ref, top_ref, bot_ref):
    tm = top_ref.shape[0]
    p = jnp.dot(cs_ref[...], a_ref[...], preferred_element_type=F32)
    q = jnp.dot(ss_ref[...], b_ref[...], preferred_element_type=F32)
    top_ref[...] = (p - q)[0:tm].astype(top_ref.dtype)
    ext = cs_ref.shape[0]
    rho = lax.broadcasted_iota(jnp.int32, (tm, ext), 0)
    col = lax.broadcasted_iota(jnp.int32, (tm, ext), 1)
    mirror = jnp.where(col == tm - rho, 1.0, 0.0).astype(BF16)
    bot_ref[...] = jnp.dot(mirror, (p + q).astype(BF16), preferred_element_type=F32).astype(bot_ref.dtype)


def _dft_seq(a, b, cos_s, sin_s, *, B, S, tm, tn):
    M, N = a.shape
    nv = S // (2 * tm)
    ext = tm + BF16_SUBLANES
    assert S % (2 * tm) == 0 and cos_s.shape == (nv * ext, S)
    half = jax.ShapeDtypeStruct((M // 2, N), BF16)
    return pl.pallas_call(
        _dft_seq_kernel,
        grid=(B, N // tn, nv),
        in_specs=[pl.BlockSpec((ext, S), lambda bb, n, v: (v, 0)),
                  pl.BlockSpec((ext, S), lambda bb, n, v: (v, 0)),
                  pl.BlockSpec((S, tn), lambda bb, n, v: (bb, n)),
                  pl.BlockSpec((S, tn), lambda bb, n, v: (bb, n))],
        out_specs=[pl.BlockSpec((tm, tn), lambda bb, n, v: (bb * nv + v, n)),
                   pl.BlockSpec((tm, tn), lambda bb, n, v: (bb * nv + nv - 1 - v, n))],
        out_shape=[half, half],
        compiler_params=_params("parallel", "parallel", "arbitrary"),
        name="dft_seq",
    )(cos_s, sin_s, a, b)


def _out_proj_kernel(o_ref, ftop_ref, fbot_ref, wa_ref, wf_ref, g_ref, x_ref, y_ref, *, tiles_per_seq):
    first_half = (pl.program_id(0) % tiles_per_seq) < tiles_per_seq // 2
    f = jnp.where(first_half, ftop_ref[...], fbot_ref[...])
    m = (jnp.dot(o_ref[...], wa_ref[...], preferred_element_type=F32)
         + jnp.dot(f, wf_ref[...], preferred_element_type=F32))
    y_ref[...] = x_ref[...] + _rms(m, g_ref[...])


def _out_proj(o, f_top, f_bot, w_out, layer, g, x, seq_len, *, tm):
    M, D = x.shape
    ka, kf = o.shape[1], f_top.shape[1]
    tps = seq_len // tm
    assert ka % kf == 0 and w_out.shape[1] == ka + kf and tps % 2 == 0
    half_tile = lambda i: ((i // tps) * (tps // 2) + i % (tps // 2), 0)
    return pl.pallas_call(
        functools.partial(_out_proj_kernel, tiles_per_seq=tps),
        grid=(M // tm,),
        in_specs=[pl.BlockSpec((tm, ka), lambda i: (i, 0)),
                  pl.BlockSpec((tm, kf), half_tile),
                  pl.BlockSpec((tm, kf), half_tile),
                  pl.BlockSpec((None, ka, D), lambda i: (layer, 0, 0)),
                  pl.BlockSpec((None, kf, D), lambda i: (layer, ka // kf, 0)),
                  pl.BlockSpec((1, D), lambda i: (0, 0)),
                  pl.BlockSpec((tm, D), lambda i: (i, 0))],
        out_specs=pl.BlockSpec((tm, D), lambda i: (i, 0)),
        out_shape=jax.ShapeDtypeStruct((M, D), F32),
        compiler_params=_params("parallel"),
        name="out_proj",
    )(o, f_top, f_bot, w_out, w_out, g.reshape(1, D), x)


def _tiles(seq_len):
    return dict(tm_block=min(512, seq_len), tf_block=512, tm_proj=min(1024, seq_len), tn_proj=1024,
                tm_out=min(512, seq_len // 2), tm_dft=min(512, seq_len // 2), tn_dft=512,
                attn_q=min(ATTN_QUERIES, seq_len), attn_k=min(ATTN_KEY_BLOCK, seq_len))


def _run_trunk(x3, p, consts):
    B, S, D = x3.shape
    x = x3.reshape(B * S, D)
    t = _tiles(S)
    depth = p["norm_pre_mix"].shape[0]
    H = p["rel_bias"].shape[1]
    head_dim = p["ab_lambda_q1"].shape[1]
    attn_width = H * 2 * head_dim
    cos_c, sin_c = consts["dft_channel"]
    cs_c = jnp.concatenate([cos_c, sin_c], axis=1)
    cos_s, sin_s = _dft_seq_tables(S, t["tm_dft"])
    for i in range(depth):
        j = i // 2
        if i % 2 == 0:
            lam_init = _lambda_init(i)
            qkv, fa, fb = _norm_in_proj(x, p["norm_pre_mix"][i], p["ab_w_in"], j, 3 * attn_width, cs_c,
                                        tm=t["tm_proj"], tn=t["tn_proj"])
            lam = (consts["lam"][j, 0] + lam_init).reshape(1, 1)
            o = _diff_attention(qkv, lam, consts["bias_tiles"], p["ab_subln"][j], B=B, S=S, H=H,
                                head_dim=head_dim, out_scale=1.0 - lam_init, T=t["attn_q"],
                                key_block=t["attn_k"])
            f_top, f_bot = _dft_seq(fa, fb, cos_s, sin_s, B=B, S=S, tm=t["tm_dft"], tn=t["tn_dft"])
            x = _out_proj(o, f_top, f_bot, p["ab_w_out"], j, p["norm_post_mix"][i], x, S, tm=t["tm_out"])
        else:
            x = _gated_block(x, S, p["norm_pre_mix"][i], p["norm_post_mix"][i], [p["c_w_in"]],
                             p["c_conv"][j], None, p["c_w_out"], j, mode="conv",
                             tm=t["tm_block"], tf=t["tf_block"])
        x = _gated_block(x, S, p["norm_pre_ffn"][i], p["norm_post_ffn"][i],
                         [p["ffn_w_gate"], p["ffn_w_up"]], p["ffn_conv"][i], p["ffn_conv_b"][i],
                         p["ffn_w_down"], i, mode="ffn", tm=t["tm_block"], tf=t["tf_block"])
    return x.reshape(B, S, D)


def kernel(x_prompt, x_sample, rel_bias, norm_pre_mix, norm_post_mix, norm_pre_ffn, norm_post_ffn, ab_w_in, ab_w_out, ab_lambda_q1, ab_lambda_k1, ab_lambda_q2, ab_lambda_k2, ab_subln, c_w_in, c_conv, c_w_out, ffn_w_gate, ffn_w_up, ffn_conv, ffn_conv_b, ffn_w_down):
    p = dict(rel_bias=rel_bias, norm_pre_mix=norm_pre_mix, norm_post_mix=norm_post_mix,
             norm_pre_ffn=norm_pre_ffn, norm_post_ffn=norm_post_ffn,
             ab_w_in=ab_w_in.astype(BF16), ab_w_out=ab_w_out.astype(BF16),
             ab_lambda_q1=ab_lambda_q1, ab_subln=ab_subln,
             c_w_in=c_w_in.astype(BF16), c_conv=c_conv, c_w_out=c_w_out.astype(BF16),
             ffn_w_gate=ffn_w_gate.astype(BF16), ffn_w_up=ffn_w_up.astype(BF16),
             ffn_conv=ffn_conv, ffn_conv_b=ffn_conv_b, ffn_w_down=ffn_w_down.astype(BF16))
    H = rel_bias.shape[1]
    attn_width = H * 2 * ab_lambda_q1.shape[1]
    fgroup = (x_prompt.shape[-1] - attn_width) // N_FGROUPS
    bias_tiles, lam = _attn_consts(rel_bias, ab_lambda_q1, ab_lambda_k1, ab_lambda_q2, ab_lambda_k2,
                                   T=ATTN_TILE)
    consts = dict(bias_tiles=bias_tiles, lam=lam,
                  dft_channel=_dft_tables(fgroup, fgroup ** -0.5))
    return (_run_trunk(x_prompt, p, consts), _run_trunk(x_sample, p, consts))
```
